```python
import math
import jax, jax.numpy as jnp
from jax import lax
import numpy as np

D_MODEL = 2048
BATCH = 4
SEQ = 2048
DEPTH = 1

MEM_LEN = 256
MIX_WIDTH = D_MODEL
MLA_HEADS = 8
QK_NOPE_DIM = 128
QK_ROPE_DIM = 64
V_HEAD_DIM = 128
Q_LORA_RANK = 384
KV_LORA_RANK = 256
ROPE_THETA = 10000.0
Q_BLOCK = 128
SSM_WIDTH = MIX_WIDTH - MLA_HEADS * V_HEAD_DIM
SSM_GROUP = 16
SSM_GROUPS = SSM_WIDTH // SSM_GROUP
SSM_STATE = 64
DT_MIN = 0.001
DT_MAX = 0.1
XATTN_HEADS = 4
XATTN_HEAD_DIM = D_MODEL // XATTN_HEADS
N_EXPERTS = 32
TOP_K = 4
D_FF = D_MODEL
SWIGLU_LIMIT = 7.0
SWIGLU_ALPHA = 1.702
EXPERT_BLOCK = 256
EPS = 1e-6
COL_Q = Q_LORA_RANK
COL_KV = COL_Q + KV_LORA_RANK
COL_KPE = COL_KV + QK_ROPE_DIM
IN_COLS = COL_KPE + SSM_WIDTH

kernel_name = 'hymba_mla_s5_xattn_moe_block'


def rms_norm(x, g):
    xf = x.astype(jnp.float32)
    xf = xf * lax.rsqrt(jnp.mean(xf * xf, axis=-1, keepdims=True) + EPS)
    return (xf * g.astype(jnp.float32)).astype(x.dtype)


def rope(x, cos, sin):
    x1, x2 = jnp.split(x.astype(jnp.float32), 2, axis=-1)
    return jnp.concatenate([x1 * cos - x2 * sin, x2 * cos + x1 * sin], axis=-1).astype(x.dtype)


def mla_group(c_q, c_kv, k_pe, positions, q_a_norm, w_q_b, kv_a_norm, w_kv_b):
    B, S, _ = c_q.shape
    H = MLA_HEADS
    q = (rms_norm(c_q, q_a_norm) @ w_q_b).reshape(B, S, H, QK_NOPE_DIM + QK_ROPE_DIM)
    q_nope, q_pe = q[..., :QK_NOPE_DIM], q[..., QK_NOPE_DIM:]
    kv = (rms_norm(c_kv, kv_a_norm) @ w_kv_b).reshape(B, S, H, QK_NOPE_DIM + V_HEAD_DIM)
    k_nope, v = kv[..., :QK_NOPE_DIM], kv[..., QK_NOPE_DIM:]
    inv_freq = 1.0 / (ROPE_THETA ** (jnp.arange(0, QK_ROPE_DIM, 2, dtype=jnp.float32) / QK_ROPE_DIM))
    ang = positions.astype(jnp.float32)[..., None] * inv_freq
    cos, sin = jnp.cos(ang), jnp.sin(ang)
    q_pe = rope(q_pe, cos[:, :, None, :], sin[:, :, None, :])
    k_pe = rope(k_pe, cos, sin)
    scale = (QK_NOPE_DIM + QK_ROPE_DIM) ** -0.5
    nb = S // Q_BLOCK

    def to_blocks(t):
        return t.reshape(B, nb, Q_BLOCK, H, t.shape[-1]).transpose(1, 0, 3, 2, 4)

    k_idx = jnp.arange(S)

    def block_attn(args):
        qn, qp, blk = args
        s = (jnp.einsum('bhqd,bkhd->bhqk', qn, k_nope)
             + jnp.einsum('bhqd,bkd->bhqk', qp, k_pe)).astype(jnp.float32) * scale
        q_idx = blk * Q_BLOCK + jnp.arange(Q_BLOCK)
        s = jnp.where(q_idx[:, None] >= k_idx[None, :], s, -jnp.inf)
        p = jax.nn.softmax(s, axis=-1).astype(v.dtype)
        return jnp.einsum('bhqk,bkhd->bhqd', p, v)

    o = lax.map(block_attn, (to_blocks(q_nope), to_blocks(q_pe), jnp.arange(nb)))
    return o.transpose(1, 0, 3, 2, 4).reshape(B, S, H * V_HEAD_DIM)


def _cscan_op(e1, e2):
    a1r, a1i, b1r, b1i = e1
    a2r, a2i, b2r, b2i = e2
    ar = a2r * a1r - a2i * a1i
    ai = a2r * a1i + a2i * a1r
    br = a2r * b1r - a2i * b1i + b2r
    bi = a2r * b1i + a2i * b1r + b2i
    return (ar, ai, br, bi)


def s5_group(u, lam_re, lam_im, log_dt, b_re, b_im, c_re, c_im, d_skip, w_glu, b_glu):
    B, S, _ = u.shape
    uf = u.astype(jnp.float32)
    lr = lam_re.astype(jnp.float32)
    li = lam_im.astype(jnp.float32)
    dt = jnp.exp(log_dt.astype(jnp.float32))[:, None]
    mag = jnp.exp(lr * dt)
    ab_re = mag * jnp.cos(li * dt)
    ab_im = mag * jnp.sin(li * dt)
    nr, ni = ab_re - 1.0, ab_im
    den = lr * lr + li * li
    coef_re = (nr * lr + ni * li) / den
    coef_im = (ni * lr - nr * li) / den
    br = b_re.astype(jnp.float32)
    bim = b_im.astype(jnp.float32)
    bb_re = coef_re[..., None] * br - coef_im[..., None] * bim
    bb_im = coef_re[..., None] * bim + coef_im[..., None] * br
    ug = uf.reshape(B, S, SSM_GROUPS, SSM_GROUP)
    bu_re = jnp.einsum('bsgc,gpc->bsgp', ug, bb_re)
    bu_im = jnp.einsum('bsgc,gpc->bsgp', ug, bb_im)
    a_re = jnp.broadcast_to(ab_re, bu_re.shape)
    a_im = jnp.broadcast_to(ab_im, bu_re.shape)
    _, _, x_re, x_im = lax.associative_scan(_cscan_op, (a_re, a_im, bu_re, bu_im), axis=1)
    y = (jnp.einsum('bsgp,gcp->bsgc', x_re, c_re.astype(jnp.float32))
         - jnp.einsum('bsgp,gcp->bsgc', x_im, c_im.astype(jnp.float32)))
    y = y.reshape(B, S, SSM_WIDTH) + d_skip.astype(jnp.float32) * uf
    y = jax.nn.gelu(y).astype(u.dtype)
    gate = jax.nn.sigmoid((y @ w_glu + b_glu).astype(jnp.float32)).astype(u.dtype)
    return y * gate


def cross_attn(h, mem_n, w_xq, w_xk, w_xv, w_xo):
    B, S, D = h.shape
    M = mem_n.shape[1]
    q = (h @ w_xq).reshape(B, S, XATTN_HEADS, XATTN_HEAD_DIM)
    k = (mem_n @ w_xk).reshape(B, M, XATTN_HEADS, XATTN_HEAD_DIM)
    v = (mem_n @ w_xv).reshape(B, M, XATTN_HEADS, XATTN_HEAD_DIM)
    s = jnp.einsum('bqhd,bkhd->bhqk', q, k).astype(jnp.float32) * XATTN_HEAD_DIM ** -0.5
    p = jax.nn.softmax(s, axis=-1).astype(v.dtype)
    o = jnp.einsum('bhqk,bkhd->bqhd', p, v).reshape(B, S, D)
    return o @ w_xo


def moe(xn, w_router, b_router, w_up, b_up, w_down, b_down):
    B, S, D = xn.shape
    T = B * S
    TK = T * TOP_K
    xt = xn.reshape(T, D)
    logits = (xt @ w_router + b_router).astype(jnp.float32)
    top_val, top_idx = lax.top_k(logits, TOP_K)
    gates = jax.nn.softmax(top_val, axis=-1)
    flat_e = top_idx.reshape(TK)
    flat_g = gates.reshape(TK)
    flat_tok = jnp.arange(TK, dtype=jnp.int32) // TOP_K
    order = jnp.argsort(flat_e)
    sorted_e = flat_e[order]
    counts = jnp.bincount(flat_e, length=N_EXPERTS)
    padded = (counts + EXPERT_BLOCK - 1) // EXPERT_BLOCK * EXPERT_BLOCK
    pad_end = jnp.cumsum(padded)
    pad_start = pad_end - padded
    start = jnp.cumsum(counts) - counts
    dest = pad_start[sorted_e] + jnp.arange(TK, dtype=jnp.int32) - start[sorted_e]
    n_blocks = -(-TK // EXPERT_BLOCK) + N_EXPERTS
    n_slots = n_blocks * EXPERT_BLOCK
    slot_tok = jnp.full((n_slots,), T, jnp.int32).at[dest].set(flat_tok[order])
    slot_gate = jnp.zeros((n_slots,), jnp.float32).at[dest].set(flat_g[order])
    block_e = jnp.minimum(
        jnp.searchsorted(pad_end, jnp.arange(n_blocks) * EXPERT_BLOCK, side='right'),
        N_EXPERTS - 1)
    x_pad = jnp.concatenate([xt, jnp.zeros((1, D), xt.dtype)], axis=0)

    def expert_block(args):
        tok, e = args
        xb = x_pad[tok]
        hu = (xb @ w_up[e] + b_up[e]).astype(jnp.float32)
        glu = jnp.minimum(hu[:, :D_FF], SWIGLU_LIMIT)
        lin = jnp.clip(hu[:, D_FF:], -SWIGLU_LIMIT, SWIGLU_LIMIT)
        act = (glu * jax.nn.sigmoid(SWIGLU_ALPHA * glu) * (lin + 1.0)).astype(xb.dtype)
        return act @ w_down[e] + b_down[e]

    out = lax.map(expert_block, (slot_tok.reshape(n_blocks, EXPERT_BLOCK), block_e))
    weighted = out.reshape(n_slots, D).astype(jnp.float32) * slot_gate[:, None]
    y = jnp.zeros((T + 1, D), jnp.float32).at[slot_tok].add(weighted)
    return y[:T].reshape(B, S, D).astype(xn.dtype)


def setup_inputs(seed: int = 0) -> dict:
    key = jax.random.key(seed)
    ks = jax.random.split(key, 40)
    f32 = jnp.float32
    L, D, G, P, C = DEPTH, D_MODEL, SSM_GROUPS, SSM_STATE, SSM_GROUP

    def w(k, shape, fan_in):
        return jax.random.normal(k, shape, f32) * fan_in ** -0.5

    def gain(k, shape):
        return 1.0 + 0.02 * jax.random.normal(k, shape, f32)

    def small(k, shape):
        return 0.01 * jax.random.normal(k, shape, f32)

    x = jax.random.normal(ks[0], (BATCH, SEQ, D), f32)
    mem = jax.random.normal(ks[1], (BATCH, MEM_LEN, D), f32)
    offsets = jax.random.randint(ks[2], (BATCH, 1), 0, 1024, dtype=jnp.int32)
    positions = (offsets + jnp.arange(SEQ, dtype=jnp.int32)[None, :]).astype(jnp.int32)
    lam_re = -0.5 + 0.01 * jax.random.normal(ks[8], (L, G, P), f32)
    lam_im = math.pi * jnp.arange(P, dtype=f32)[None, None, :] + 0.01 * jax.random.normal(ks[9], (L, G, P), f32)
    log_dt = jax.random.uniform(ks[10], (L, G), f32, math.log(DT_MIN), math.log(DT_MAX))
    return {
        'x': x,
        'mem': mem,
        'positions': positions,
        'attn_norm': gain(ks[3], (L, D)),
        'w_in': w(ks[4], (L, D, IN_COLS), D),
        'q_a_norm': gain(ks[5], (L, Q_LORA_RANK)),
        'w_q_b': w(ks[6], (L, Q_LORA_RANK, MLA_HEADS * (QK_NOPE_DIM + QK_ROPE_DIM)), Q_LORA_RANK),
        'kv_a_norm': gain(ks[7], (L, KV_LORA_RANK)),
        'w_kv_b': w(ks[11], (L, KV_LORA_RANK, MLA_HEADS * (QK_NOPE_DIM + V_HEAD_DIM)), KV_LORA_RANK),
        'ssm_lambda_re': lam_re,
        'ssm_lambda_im': lam_im,
        'ssm_log_dt': log_dt,
        'ssm_b_re': w(ks[12], (L, G, P, C), 2 * C),
        'ssm_b_im': w(ks[13], (L, G, P, C), 2 * C),
        'ssm_c_re': w(ks[14], (L, G, C, P), 2 * P),
        'ssm_c_im': w(ks[15], (L, G, C, P), 2 * P),
        'ssm_d': 0.1 * jax.random.normal(ks[16], (L, SSM_WIDTH), f32),
        'w_glu': w(ks[17], (L, SSM_WIDTH, SSM_WIDTH), SSM_WIDTH),
        'b_glu': small(ks[18], (L, SSM_WIDTH)),
        'mix_norm_attn': gain(ks[19], (L, MLA_HEADS * V_HEAD_DIM)),
        'mix_norm_ssm': gain(ks[20], (L, SSM_WIDTH)),
        'w_out': w(ks[21], (L, MIX_WIDTH, D), MIX_WIDTH),
        'xattn_norm': gain(ks[22], (L, D)),
        'mem_norm': gain(ks[23], (L, D)),
        'w_xq': w(ks[24], (L, D, D), D),
        'w_xk': w(ks[25], (L, D, D), D),
        'w_xv': w(ks[26], (L, D, D), D),
        'w_xo': w(ks[27], (L, D, D), D),
        'ffn_norm': gain(ks[28], (L, D)),
        'w_router': w(ks[29], (L, D, N_EXPERTS), D),
        'b_router': small(ks[30], (L, N_EXPERTS)),
        'w_up': w(ks[31], (L, N_EXPERTS, D, 2 * D_FF), D),
        'b_up': small(ks[32], (L, N_EXPERTS, 2 * D_FF)),
        'w_down': w(ks[33], (L, N_EXPERTS, D_FF, D), D_FF),
        'b_down': small(ks[34], (L, N_EXPERTS, D)),
        'final_norm': gain(ks[35], (D,)),
    }


def reference(x, mem, positions, attn_norm, w_in, q_a_norm, w_q_b, kv_a_norm, w_kv_b,
              ssm_lambda_re, ssm_lambda_im, ssm_log_dt, ssm_b_re, ssm_b_im, ssm_c_re, ssm_c_im,
              ssm_d, w_glu, b_glu, mix_norm_attn, mix_norm_ssm, w_out,
              xattn_norm, mem_norm, w_xq, w_xk, w_xv, w_xo,
              ffn_norm, w_router, b_router, w_up, b_up, w_down, b_down, final_norm):
    for l in range(DEPTH):
        h = rms_norm(x, attn_norm[l])
        proj = h @ w_in[l]
        c_q = proj[..., :COL_Q]
        c_kv = proj[..., COL_Q:COL_KV]
        k_pe = proj[..., COL_KV:COL_KPE]
        u = proj[..., COL_KPE:]
        attn_o = mla_group(c_q, c_kv, k_pe, positions, q_a_norm[l], w_q_b[l], kv_a_norm[l], w_kv_b[l])
        ssm_o = s5_group(u, ssm_lambda_re[l], ssm_lambda_im[l], ssm_log_dt[l], ssm_b_re[l], ssm_b_im[l],
                         ssm_c_re[l], ssm_c_im[l], ssm_d[l], w_glu[l], b_glu[l])
        mixed = jnp.concatenate([rms_norm(attn_o, mix_norm_attn[l]), rms_norm(ssm_o, mix_norm_ssm[l])], axis=-1)
        x = x + mixed @ w_out[l]
        x = x + cross_attn(rms_norm(x, xattn_norm[l]), rms_norm(mem, mem_norm[l]),
                           w_xq[l], w_xk[l], w_xv[l], w_xo[l])
        x = x + moe(rms_norm(x, ffn_norm[l]), w_router[l], b_router[l], w_up[l], b_up[l], w_down[l], b_down[l])
    return rms_norm(x, final_norm)
```

```python
import functools
import math

import jax
import jax.numpy as jnp
from jax import lax
from jax.experimental import pallas as pl
from jax.experimental.pallas import tpu as pltpu

F32 = jnp.float32
BF16 = jnp.bfloat16
I32 = jnp.int32

D_MODEL = 2048
MEM_LEN = 256
MLA_HEADS = 8
QK_NOPE_DIM = 128
QK_ROPE_DIM = 64
V_HEAD_DIM = 128
Q_LORA_RANK = 384
KV_LORA_RANK = 256
ROPE_THETA = 10000.0
SSM_WIDTH = 1024
SSM_GROUP = 16
SSM_GROUPS = 64
SSM_STATE = 64
XATTN_HEADS = 4
XATTN_HEAD_DIM = D_MODEL // XATTN_HEADS
N_EXPERTS = 32
TOP_K = 4
D_FF = D_MODEL
SWIGLU_LIMIT = 7.0
SWIGLU_ALPHA = 1.702
EPS = 1e-6
COL_Q = Q_LORA_RANK
COL_KV = COL_Q + KV_LORA_RANK
COL_KPE = COL_KV + QK_ROPE_DIM

LANES = 128
SUBLANES = 8
MXU_DIM = 256
VMEM_LIMIT_BYTES = 56 * 1024 * 1024

HEAD_SLOT = 2 * LANES
STATE_ELEMS = SSM_GROUPS * SSM_STATE
STATE_SLABS = STATE_ELEMS // (SUBLANES * LANES)

MOE_ROW_TILE = 256
MOE_UNIT_ROWS = 2048
MOE_FF_CHUNK = 256
MOE_NJ = D_FF // MOE_FF_CHUNK
COMBINE_ROWS = 256


def _cparams(semantics):
    return pltpu.CompilerParams(dimension_semantics=semantics,
                                vmem_limit_bytes=VMEM_LIMIT_BYTES)


def _resident(shape):
    nd = len(shape)
    return pl.BlockSpec(shape, lambda *_: (0,) * nd, pipeline_mode=pl.Buffered(1))


def _rms(xf, g):
    ms = jnp.mean(xf * xf, axis=-1, keepdims=True)
    return xf * lax.rsqrt(ms + EPS) * g


def _proj_body(x_ref, pos_ref, g_ref, w1_ref, qn_ref, wqb_ref, kvn_ref, wkvb_ref, freq_ref,
               q_ref, k_ref, v_ref, u_ref, *, scale):
    h = _rms(x_ref[...], g_ref[...]).astype(BF16)
    p = jnp.dot(h, w1_ref[...], preferred_element_type=F32)
    o_kv, o_ka, o_kb, o_u = COL_Q, COL_KV, COL_KV + LANES, COL_KV + 2 * LANES
    cq = p[:, :o_kv]
    ckv = p[:, o_kv:o_ka]
    kpa = p[:, o_ka:o_kb]
    kpb = p[:, o_kb:o_u]
    u_ref[...] = p[:, o_u:]
    ang = pos_ref[...].astype(F32) * freq_ref[...]
    c = jnp.cos(ang)
    s = jnp.sin(ang)
    qq = jnp.dot(_rms(cq, qn_ref[...]).astype(BF16), wqb_ref[...], preferred_element_type=F32)
    kv = jnp.dot(_rms(ckv, kvn_ref[...]).astype(BF16), wkvb_ref[...], preferred_element_type=F32)
    kpe = (kpa * c + kpb * s).astype(BF16)
    swap0 = MLA_HEADS * HEAD_SLOT
    for hd in range(MLA_HEADS):
        a0 = hd * HEAD_SLOT
        q_ref[:, a0:a0 + LANES] = (qq[:, a0:a0 + LANES] * scale).astype(BF16)
        qa = qq[:, a0 + LANES:a0 + HEAD_SLOT]
        qb = qq[:, swap0 + hd * LANES:swap0 + (hd + 1) * LANES]
        q_ref[:, a0 + LANES:a0 + HEAD_SLOT] = ((qa * c + qb * s) * scale).astype(BF16)
        k_ref[:, a0:a0 + LANES] = kv[:, hd * LANES:(hd + 1) * LANES].astype(BF16)
        k_ref[:, a0 + LANES:a0 + HEAD_SLOT] = kpe
    v_ref[...] = kv[:, MLA_HEADS * QK_NOPE_DIM:].astype(BF16)


def _rope_slot_pair(w):
    half = QK_ROPE_DIM // 2
    pad = jnp.zeros(w.shape[:-1] + (LANES - QK_ROPE_DIM,), w.dtype)
    a = jnp.concatenate([w, pad], axis=-1)
    b = jnp.concatenate([-w[..., half:], w[..., :half], pad], axis=-1)
    return a, b


def _proj_call(x2d, pos2d, attn_norm, w_in, q_a_norm, w_q_b, kv_a_norm, w_kv_b, tm=256):
    T = x2d.shape[0]
    kpa, kpb = _rope_slot_pair(w_in[:, COL_KV:COL_KPE])
    w1 = jnp.concatenate([w_in[:, :COL_KV], kpa, kpb, w_in[:, COL_KPE:]], axis=1).astype(BF16)
    wq = w_q_b.reshape(Q_LORA_RANK, MLA_HEADS, QK_NOPE_DIM + QK_ROPE_DIM)
    qpa, qpb = _rope_slot_pair(wq[..., QK_NOPE_DIM:])
    wqb = jnp.concatenate([
        jnp.concatenate([wq[..., :QK_NOPE_DIM], qpa], axis=-1).reshape(Q_LORA_RANK, MLA_HEADS * HEAD_SLOT),
        qpb.reshape(Q_LORA_RANK, MLA_HEADS * LANES)], axis=1).astype(BF16)
    wkv = w_kv_b.reshape(KV_LORA_RANK, MLA_HEADS, QK_NOPE_DIM + V_HEAD_DIM)
    wkvb = jnp.concatenate([wkv[..., :QK_NOPE_DIM].reshape(KV_LORA_RANK, -1),
                            wkv[..., QK_NOPE_DIM:].reshape(KV_LORA_RANK, -1)], axis=1).astype(BF16)
    inv_freq = 1.0 / (ROPE_THETA ** (jnp.arange(0, QK_ROPE_DIM, 2, dtype=F32) / QK_ROPE_DIM))
    freq = jnp.concatenate([inv_freq, inv_freq, jnp.zeros((LANES - QK_ROPE_DIM,), F32)])[None, :]
    scale = (QK_NOPE_DIM + QK_ROPE_DIM) ** -0.5
    n1 = w1.shape[1]
    row = lambda w: pl.BlockSpec((tm, w), lambda i: (i, 0))
    return pl.pallas_call(
        functools.partial(_proj_body, scale=scale),
        grid=(T // tm,),
        in_specs=[row(D_MODEL), row(1), _resident((1, D_MODEL)), _resident((D_MODEL, n1)),
                  _resident((1, Q_LORA_RANK)), _resident(wqb.shape),
                  _resident((1, KV_LORA_RANK)), _resident(wkvb.shape), _resident((1, LANES))],
        out_specs=[row(MLA_HEADS * HEAD_SLOT), row(MLA_HEADS * HEAD_SLOT),
                   row(MLA_HEADS * V_HEAD_DIM), row(SSM_WIDTH)],
        out_shape=[jax.ShapeDtypeStruct((T, MLA_HEADS * HEAD_SLOT), BF16),
                   jax.ShapeDtypeStruct((T, MLA_HEADS * HEAD_SLOT), BF16),
                   jax.ShapeDtypeStruct((T, MLA_HEADS * V_HEAD_DIM), BF16),
                   jax.ShapeDtypeStruct((T, SSM_WIDTH), F32)],
        compiler_params=_cparams(("parallel",)),
        name="proj_mla_rope",
    )(x2d, pos2d, attn_norm[None, :], w1, q_a_norm[None, :], wqb, kv_a_norm[None, :], wkvb, freq)


def _attn_body(q_ref, k_ref, v_ref, o_ref, *, tq):
    i = pl.program_id(2)
    q = q_ref[...]

    def step(j, carry, diagonal):
        m, l, acc = carry
        off = pl.multiple_of(j * tq, tq)
        kj = k_ref[pl.ds(off, tq), :]
        vj = v_ref[pl.ds(off, tq), :]
        s = lax.dot_general(q, kj, (((1,), (1,)), ((), ())), preferred_element_type=F32)
        if diagonal:
            rows = lax.broadcasted_iota(I32, s.shape, 0)
            cols = lax.broadcasted_iota(I32, s.shape, 1)
            s = jnp.where(rows >= cols, s, -jnp.inf)
        m_new = jnp.maximum(m, jnp.max(s, axis=-1, keepdims=True))
        alpha = jnp.exp(m - m_new)
        p = jnp.exp(s - m_new)
        l = alpha * l + jnp.sum(p, axis=-1, keepdims=True)
        acc = alpha * acc + jnp.dot(p.astype(BF16), vj, preferred_element_type=F32)
        return m_new, l, acc

    init = (jnp.full((tq, 1), -jnp.inf, F32), jnp.zeros((tq, 1), F32),
            jnp.zeros((tq, V_HEAD_DIM), F32))
    carry = lax.fori_loop(0, i, lambda j, c: step(j, c, False), init)
    _, l, acc = step(i, carry, True)
    o_ref[...] = acc / l


def _attn_call(q, k, v, batch, seq, tq=512):
    nq = seq // tq
    return pl.pallas_call(
        functools.partial(_attn_body, tq=tq),
        grid=(batch, MLA_HEADS, nq),
        in_specs=[pl.BlockSpec((tq, HEAD_SLOT), lambda b, h, i: (b * nq + i, h)),
                  pl.BlockSpec((seq, HEAD_SLOT), lambda b, h, i: (b, h)),
                  pl.BlockSpec((seq, V_HEAD_DIM), lambda b, h, i: (b, h))],
        out_specs=pl.BlockSpec((tq, V_HEAD_DIM), lambda b, h, i: (b * nq + i, h)),
        out_shape=jax.ShapeDtypeStruct((batch * seq, MLA_HEADS * V_HEAD_DIM), F32),
        compiler_params=_cparams(("parallel", "parallel", "arbitrary")),
        name="mla_attention",
    )(q, k, v)


def _s5_param_body(lr_ref, li_ref, ldt_ref, br_ref, bi_ref, abr_ref, abi_ref, bbr_ref, bbi_ref):
    lr = lr_ref[...]
    li = li_ref[...]
    dt = jnp.exp(ldt_ref[...])
    mag = jnp.exp(lr * dt)
    ab_re = mag * jnp.cos(li * dt)
    ab_im = mag * jnp.sin(li * dt)
    nr, ni = ab_re - 1.0, ab_im
    den = lr * lr + li * li
    coef_re = (nr * lr + ni * li) / den
    coef_im = (ni * lr - nr * li) / den
    abr_ref[...] = ab_re
    abi_ref[...] = ab_im
    for c in range(SSM_GROUP):
        br = br_ref[c]
        bi = bi_ref[c]
        bbr_ref[c] = coef_re * br - coef_im * bi
        bbi_ref[c] = coef_re * bi + coef_im * br


def _s5_param_call(lam_re, lam_im, log_dt, b_re, b_im):
    G, P, C = SSM_GROUPS, SSM_STATE, SSM_GROUP
    gp = jax.ShapeDtypeStruct((G, P), F32)
    cgp = jax.ShapeDtypeStruct((C, G, P), F32)
    return pl.pallas_call(
        _s5_param_body, out_shape=[gp, gp, cgp, cgp], name="s5_discretise",
    )(lam_re, lam_im, log_dt[:, None], b_re.transpose(2, 0, 1), b_im.transpose(2, 0, 1))


def _s5_body(u_ref, bre_ref, bim_ref, cre_ref, cim_ref, ar_ref, ai_ref, d_ref, wg_ref, bg_ref,
             o_ref, sre, sim, st_re, st_im, *, tt):
    nk = SSM_WIDTH // MXU_DIM
    cols_per_k = STATE_ELEMS // nk
    lane_blocks = cols_per_k // LANES
    c_idx = pl.program_id(1)

    @pl.when(c_idx == 0)
    def _():
        st_re[...] = jnp.zeros_like(st_re)
        st_im[...] = jnp.zeros_like(st_im)

    u = u_ref[...]
    ub = u.astype(BF16)
    for k in range(nk):
        uk = ub[:, k * MXU_DIM:(k + 1) * MXU_DIM]
        pr = jnp.dot(uk, bre_ref[k], preferred_element_type=F32)
        pi = jnp.dot(uk, bim_ref[k], preferred_element_type=F32)
        for m in range(lane_blocks):
            blk = k * lane_blocks + m
            s_, j_ = blk // STATE_SLABS, blk % STATE_SLABS
            sre[j_, pl.ds(s_, tt, stride=SUBLANES), :] = pr[:, m * LANES:(m + 1) * LANES]
            sim[j_, pl.ds(s_, tt, stride=SUBLANES), :] = pi[:, m * LANES:(m + 1) * LANES]

    ar = [ar_ref[j] for j in range(STATE_SLABS)]
    ai = [ai_ref[j] for j in range(STATE_SLABS)]

    def scan_step(t, carry):
        xr, xi = carry
        row = pl.multiple_of(t * SUBLANES, SUBLANES)
        nxr, nxi = [], []
        for j in range(STATE_SLABS):
            br = sre[j, pl.ds(row, SUBLANES), :]
            bi = sim[j, pl.ds(row, SUBLANES), :]
            r = ar[j] * xr[j] - ai[j] * xi[j] + br
            im = ar[j] * xi[j] + ai[j] * xr[j] + bi
            sre[j, pl.ds(row, SUBLANES), :] = r
            sim[j, pl.ds(row, SUBLANES), :] = im
            nxr.append(r)
            nxi.append(im)
        return tuple(nxr), tuple(nxi)

    x0 = (tuple(st_re[j] for j in range(STATE_SLABS)), tuple(st_im[j] for j in range(STATE_SLABS)))
    xr, xi = lax.fori_loop(0, tt, scan_step, x0, unroll=8)
    for j in range(STATE_SLABS):
        st_re[j] = xr[j]
        st_im[j] = xi[j]

    ys = []
    for k in range(nk):
        xre, xim = [], []
        for m in range(lane_blocks):
            blk = k * lane_blocks + m
            s_, j_ = blk // STATE_SLABS, blk % STATE_SLABS
            xre.append(sre[j_, pl.ds(s_, tt, stride=SUBLANES), :])
            xim.append(sim[j_, pl.ds(s_, tt, stride=SUBLANES), :])
        xre = jnp.concatenate(xre, axis=1).astype(BF16)
        xim = jnp.concatenate(xim, axis=1).astype(BF16)
        ys.append(jnp.dot(xre, cre_ref[k], preferred_element_type=F32)
                  - jnp.dot(xim, cim_ref[k], preferred_element_type=F32))
    y = jnp.concatenate(ys, axis=1) + d_ref[...] * u
    y = jax.nn.gelu(y)
    gate = jax.nn.sigmoid(jnp.dot(y.astype(BF16), wg_ref[...], preferred_element_type=F32) + bg_ref[...])
    o_ref[...] = y * gate


def _block_diag(blocks):
    nk, ng, r, c = blocks.shape
    eye = jnp.eye(ng, dtype=blocks.dtype)
    out = blocks[:, :, :, None, :] * eye[None, :, None, :, None]
    return out.reshape(nk, ng * r, ng * c)


def _s5_call(u, ab_re, ab_im, bb_re, bb_im, c_re, c_im, d_skip, w_glu, b_glu, batch, seq, tt=256):
    G, P, C = SSM_GROUPS, SSM_STATE, SSM_GROUP
    nk = SSM_WIDTH // MXU_DIM
    gk = G // nk
    def bmat(bb):
        blocks = bb.transpose(1, 0, 2).reshape(nk, gk, C, P)
        return _block_diag(blocks).astype(BF16)
    def cmat(cc):
        blocks = cc.transpose(0, 2, 1).reshape(nk, gk, P, C)
        return _block_diag(blocks).astype(BF16)
    def slab(a):
        return a.reshape(SUBLANES, STATE_SLABS, LANES).transpose(1, 0, 2)
    nc = seq // tt
    row = pl.BlockSpec((tt, SSM_WIDTH), lambda b, c: (b * nc + c, 0))
    return pl.pallas_call(
        functools.partial(_s5_body, tt=tt),
        grid=(batch, nc),
        in_specs=[row, _resident((nk, MXU_DIM, STATE_ELEMS // nk)), _resident((nk, MXU_DIM, STATE_ELEMS // nk)),
                  _resident((nk, STATE_ELEMS // nk, MXU_DIM)), _resident((nk, STATE_ELEMS // nk, MXU_DIM)),
                  _resident((STATE_SLABS, SUBLANES, LANES)), _resident((STATE_SLABS, SUBLANES, LANES)),
                  _resident((1, SSM_WIDTH)), _resident((SSM_WIDTH, SSM_WIDTH)), _resident((1, SSM_WIDTH))],
        out_specs=row,
        out_shape=jax.ShapeDtypeStruct((batch * seq, SSM_WIDTH), F32),
        scratch_shapes=[pltpu.VMEM((STATE_SLABS, tt * SUBLANES, LANES), F32),
                        pltpu.VMEM((STATE_SLABS, tt * SUBLANES, LANES), F32),
                        pltpu.VMEM((STATE_SLABS, SUBLANES, LANES), F32),
                        pltpu.VMEM((STATE_SLABS, SUBLANES, LANES), F32)],
        compiler_params=_cparams(("parallel", "arbitrary")),
        name="s5_scan_glu",
    )(u, bmat(bb_re), bmat(bb_im), cmat(c_re), cmat(c_im), slab(ab_re), slab(ab_im),
      d_skip[None, :], w_glu.astype(BF16), b_glu[None, :])


def _mix_body(x_ref, ao_ref, so_ref, ga_ref, gs_ref, woa_ref, wos_ref, gx_ref, wxq_ref,
              x1_ref, qx_ref, *, xscale):
    a = _rms(ao_ref[...], ga_ref[...]).astype(BF16)
    s = _rms(so_ref[...], gs_ref[...]).astype(BF16)
    x1 = (x_ref[...] + jnp.dot(a, woa_ref[...], preferred_element_type=F32)
          + jnp.dot(s, wos_ref[...], preferred_element_type=F32))
    x1_ref[...] = x1
    hq = _rms(x1, gx_ref[...]).astype(BF16)
    qx_ref[...] = (jnp.dot(hq, wxq_ref[...], preferred_element_type=F32) * xscale).astype(BF16)


def _mix_call(x2d, attn_o, ssm_o, g_attn, g_ssm, w_out, g_x, w_xq, tm=256):
    T = x2d.shape[0]
    na = MLA_HEADS * V_HEAD_DIM
    row = lambda w: pl.BlockSpec((tm, w), lambda i: (i, 0))
    return pl.pallas_call(
        functools.partial(_mix_body, xscale=XATTN_HEAD_DIM ** -0.5),
        grid=(T // tm,),
        in_specs=[row(D_MODEL), row(na), row(SSM_WIDTH), _resident((1, na)), _resident((1, SSM_WIDTH)),
                  _resident((na, D_MODEL)), _resident((SSM_WIDTH, D_MODEL)),
                  _resident((1, D_MODEL)), _resident((D_MODEL, D_MODEL))],
        out_specs=[row(D_MODEL), row(D_MODEL)],
        out_shape=[jax.ShapeDtypeStruct((T, D_MODEL), F32), jax.ShapeDtypeStruct((T, D_MODEL), BF16)],
        compiler_params=_cparams(("parallel",)),
        name="mix_out_proj",
    )(x2d, attn_o, ssm_o, g_attn[None, :], g_ssm[None, :], w_out[:na].astype(BF16),
      w_out[na:].astype(BF16), g_x[None, :], w_xq.astype(BF16))


def _memkv_body(m_ref, g_ref, wk_ref, wv_ref, k_ref, v_ref):
    h = _rms(m_ref[...], g_ref[...]).astype(BF16)
    k_ref[...] = jnp.dot(h, wk_ref[...], preferred_element_type=F32).astype(BF16)
    v_ref[...] = jnp.dot(h, wv_ref[...], preferred_element_type=F32).astype(BF16)


def _memkv_call(mem2d, g_mem, w_xk, w_xv, tm=256):
    R = mem2d.shape[0]
    row = pl.BlockSpec((tm, D_MODEL), lambda i: (i, 0))
    return pl.pallas_call(
        _memkv_body,
        grid=(R // tm,),
        in_specs=[row, _resident((1, D_MODEL)), _resident((D_MODEL, D_MODEL)), _resident((D_MODEL, D_MODEL))],
        out_specs=[row, row],
        out_shape=[jax.ShapeDtypeStruct((R, D_MODEL), BF16)] * 2,
        compiler_params=_cparams(("parallel",)),
        name="mem_kv_proj",
    )(mem2d, g_mem[None, :], w_xk.astype(BF16), w_xv.astype(BF16))


def _xattn_body(q_ref, k_ref, v_ref, x1_ref, wo_ref, gf_ref, x2_ref, xn_ref):
    outs = []
    for h in range(XATTN_HEADS):
        sl = slice(h * XATTN_HEAD_DIM, (h + 1) * XATTN_HEAD_DIM)
        s = lax.dot_general(q_ref[:, sl], k_ref[:, sl], (((1,), (1,)), ((), ())),
                            preferred_element_type=F32)
        p = jnp.exp(s - jnp.max(s, axis=-1, keepdims=True))
        denom = jnp.sum(p, axis=-1, keepdims=True)
        o = jnp.dot(p.astype(BF16), v_ref[:, sl], preferred_element_type=F32) / denom
        outs.append(o.astype(BF16))
    o = jnp.concatenate(outs, axis=1)
    x2 = x1_ref[...] + jnp.dot(o, wo_ref[...], preferred_element_type=F32)
    x2_ref[...] = x2
    xn_ref[...] = _rms(x2, gf_ref[...])


def _xattn_call(qx, kx, vx, x1, w_xo, g_ffn, batch, seq, tq=256):
    nq = seq // tq
    row = pl.BlockSpec((tq, D_MODEL), lambda b, i: (b * nq + i, 0))
    memb = pl.BlockSpec((MEM_LEN, D_MODEL), lambda b, i: (b, 0))
    return pl.pallas_call(
        _xattn_body,
        grid=(batch, nq),
        in_specs=[row, memb, memb, row, _resident((D_MODEL, D_MODEL)), _resident((1, D_MODEL))],
        out_specs=[row, row],
        out_shape=[jax.ShapeDtypeStruct((batch * seq, D_MODEL), F32)] * 2,
        compiler_params=_cparams(("parallel", "parallel")),
        name="cross_attention",
    )(qx, kx, vx, x1, w_xo.astype(BF16), g_ffn[None, :])


def _router_body(xn_ref, wr_ref, br_ref, idx_ref, gate_ref, rank_ref, cnt_ref, carry, *, tm):
    @pl.when(pl.program_id(0) == 0)
    def _():
        carry[...] = jnp.zeros_like(carry)

    logits = jnp.dot(xn_ref[...], wr_ref[...], preferred_element_type=F32,
                     precision=lax.Precision.HIGHEST) + br_ref[...]
    lane = lax.broadcasted_iota(I32, logits.shape, 1)
    work = logits
    sels, vals, ids = [], [], []
    for _ in range(TOP_K):
        mx = jnp.max(work, axis=-1, keepdims=True)
        am = jnp.min(jnp.where(work == mx, lane, N_EXPERTS), axis=-1, keepdims=True)
        sel = lane == am
        sels.append(sel)
        vals.append(mx)
        ids.append(am)
        work = jnp.where(sel, -jnp.inf, work)
    es = [jnp.exp(v - vals[0]) for v in vals]
    denom = es[0] + es[1] + es[2] + es[3]
    member = jnp.zeros(logits.shape, F32)
    for sel in sels:
        member = member + jnp.where(sel, 1.0, 0.0)
    r_i = lax.broadcasted_iota(I32, (tm, tm), 0)
    c_i = lax.broadcasted_iota(I32, (tm, tm), 1)
    tri = jnp.where(r_i > c_i, 1.0, 0.0).astype(BF16)
    before = jnp.dot(tri, member.astype(BF16), preferred_element_type=F32) + carry[...]
    out_lane = lax.broadcasted_iota(I32, (tm, LANES), 1)
    idx_o = jnp.zeros((tm, LANES), I32)
    gate_o = jnp.zeros((tm, LANES), F32)
    rank_o = jnp.zeros((tm, LANES), I32)
    for k in range(TOP_K):
        rk = jnp.sum(jnp.where(sels[k], before, 0.0), axis=-1, keepdims=True).astype(I32)
        idx_o = jnp.where(out_lane == k, ids[k], idx_o)
        gate_o = jnp.where(out_lane == k, es[k] / denom, gate_o)
        rank_o = jnp.where(out_lane == k, rk, rank_o)
    idx_ref[...] = idx_o
    gate_ref[...] = gate_o
    rank_ref[...] = rank_o
    carry[...] = carry[...] + jnp.sum(member, axis=0, keepdims=True)
    cnt_ref[...] = carry[...]


def _router_call(xn, w_router, b_router, tm=256):
    T = xn.shape[0]
    wide = pl.BlockSpec((tm, LANES), lambda i: (i, 0))
    return pl.pallas_call(
        functools.partial(_router_body, tm=tm),
        grid=(T // tm,),
        in_specs=[pl.BlockSpec((tm, D_MODEL), lambda i: (i, 0)),
                  _resident((D_MODEL, N_EXPERTS)), _resident((1, N_EXPERTS))],
        out_specs=[wide, wide, wide, pl.BlockSpec((1, N_EXPERTS), lambda i: (0, 0))],
        out_shape=[jax.ShapeDtypeStruct((T, LANES), I32), jax.ShapeDtypeStruct((T, LANES), F32),
                   jax.ShapeDtypeStruct((T, LANES), I32), jax.ShapeDtypeStruct((1, N_EXPERTS), F32)],
        scratch_shapes=[pltpu.VMEM((1, N_EXPERTS), F32)],
        compiler_params=_cparams(("arbitrary",)),
        name="moe_router",
    )(xn, w_router, b_router[None, :])


def _expert_body(ue_ref, ur_ref, us_ref, tail_ref, tok_hbm, xn_hbm, wg_ref, wl_ref, bg_ref, bl_ref,
                 wd_ref, bd_ref, out_hbm, tok_smem, stage, acc, wg_bf, wl_bf, wd_bf, sems):
    u = pl.program_id(0)
    j = pl.program_id(1)
    rows = ur_ref[u]
    s0 = us_ref[u]
    nsub = (rows + (MOE_ROW_TILE - 1)) // MOE_ROW_TILE
    tok_rows = MOE_UNIT_ROWS // MOE_ROW_TILE

    @pl.when(jnp.logical_and(u == 0, j == 0))
    def _():
        stage[...] = jnp.zeros_like(stage)

    @pl.when(jnp.logical_and(j == 0, rows > 0))
    def _():
        cp = pltpu.make_async_copy(tok_hbm.at[pl.ds(s0 // MOE_ROW_TILE, tok_rows)], tok_smem, sems.at[0])
        cp.start()
        cp.wait()

        def row_copy(r, tok):
            return pltpu.make_async_copy(xn_hbm.at[pl.ds(tok, 1)], stage.at[pl.ds(r, 1)], sems.at[1])

        def issue(r, c):
            tok = tok_smem[r // MOE_ROW_TILE, 0, r % MOE_ROW_TILE]
            row_copy(r, tok).start()
            return c

        lax.fori_loop(0, rows, issue, 0)

        def drain(r, c):
            row_copy(0, 0).wait()
            return c

        lax.fori_loop(0, rows, drain, 0)
        bd = jnp.broadcast_to(bd_ref[0], (MOE_ROW_TILE, D_MODEL))

        def init(r, c):
            off = pl.multiple_of(r * MOE_ROW_TILE, MOE_ROW_TILE)
            acc[pl.ds(off, MOE_ROW_TILE), :] = bd
            return c

        lax.fori_loop(0, nsub, init, 0)

    wg_bf[...] = wg_ref[0].astype(BF16)
    wl_bf[...] = wl_ref[0].astype(BF16)
    wd_bf[...] = wd_ref[0].astype(BF16)
    bg = bg_ref[0]
    bl = bl_ref[0]

    def sub(r, c):
        off = pl.multiple_of(r * MOE_ROW_TILE, MOE_ROW_TILE)
        xs = stage[pl.ds(off, MOE_ROW_TILE), :].astype(BF16)
        g = jnp.dot(xs, wg_bf[...], preferred_element_type=F32) + bg
        l = jnp.dot(xs, wl_bf[...], preferred_element_type=F32) + bl
        glu = jnp.minimum(g, SWIGLU_LIMIT)
        lin = jnp.clip(l, -SWIGLU_LIMIT, SWIGLU_LIMIT)
        act = (glu * jax.nn.sigmoid(SWIGLU_ALPHA * glu) * (lin + 1.0)).astype(BF16)
        acc[pl.ds(off, MOE_ROW_TILE), :] += jnp.dot(act, wd_bf[...], preferred_element_type=F32)
        return c

    lax.fori_loop(0, nsub, sub, 0)

    @pl.when(j == MOE_NJ - 1)
    def _():
        def out_copy(r):
            off = pl.multiple_of(r * MOE_ROW_TILE, MOE_ROW_TILE)
            dst = pl.multiple_of(s0 + r * MOE_ROW_TILE, MOE_ROW_TILE)
            return pltpu.make_async_copy(acc.at[pl.ds(off, MOE_ROW_TILE)],
                                         out_hbm.at[pl.ds(dst, MOE_ROW_TILE)], sems.at[2])

        def start(r, c):
            out_copy(r).start()
            return c

        def wait(r, c):
            out_copy(r).wait()
            return c

        lax.fori_loop(0, nsub, start, 0)
        lax.fori_loop(0, nsub, wait, 0)

    @pl.when(jnp.logical_and(u == pl.num_programs(0) - 1, j == MOE_NJ - 1))
    def _():
        acc[pl.ds(0, MOE_ROW_TILE), :] = jnp.zeros((MOE_ROW_TILE, D_MODEL), F32)
        first = tail_ref[0]
        count = tail_ref[1]

        def zero_copy(t):
            dst = pl.multiple_of((first + t) * MOE_ROW_TILE, MOE_ROW_TILE)
            return pltpu.make_async_copy(acc.at[pl.ds(0, MOE_ROW_TILE)],
                                         out_hbm.at[pl.ds(dst, MOE_ROW_TILE)], sems.at[2])

        def start(t, c):
            zero_copy(t).start()
            return c

        def wait(t, c):
            zero_copy(t).wait()
            return c

        lax.fori_loop(0, count, start, 0)
        lax.fori_loop(0, count, wait, 0)


def _expert_call(unit_e, unit_rows, unit_s0, tail, slot_tok3, xn, w_up, b_up, w_down, b_down, n_slots, n_units):
    fc = MOE_FF_CHUNK
    grid_spec = pltpu.PrefetchScalarGridSpec(
        num_scalar_prefetch=4,
        grid=(n_units, MOE_NJ),
        in_specs=[
            pl.BlockSpec(memory_space=pl.ANY),
            pl.BlockSpec(memory_space=pl.ANY),
            pl.BlockSpec((1, D_MODEL, fc), lambda u, j, ue, ur, us, tl: (ue[u], 0, j)),
            pl.BlockSpec((1, D_MODEL, fc), lambda u, j, ue, ur, us, tl: (ue[u], 0, MOE_NJ + j)),
            pl.BlockSpec((1, 1, fc), lambda u, j, ue, ur, us, tl: (ue[u], 0, j)),
            pl.BlockSpec((1, 1, fc), lambda u, j, ue, ur, us, tl: (ue[u], 0, MOE_NJ + j)),
            pl.BlockSpec((1, fc, D_MODEL), lambda u, j, ue, ur, us, tl: (ue[u], j, 0)),
            pl.BlockSpec((1, 1, D_MODEL), lambda u, j, ue, ur, us, tl: (ue[u], 0, 0)),
        ],
        out_specs=pl.BlockSpec(memory_space=pl.ANY),
        scratch_shapes=[
            pltpu.SMEM((MOE_UNIT_ROWS // MOE_ROW_TILE, 1, MOE_ROW_TILE), I32),
            pltpu.VMEM((MOE_UNIT_ROWS, D_MODEL), F32),
            pltpu.VMEM((MOE_UNIT_ROWS, D_MODEL), F32),
            pltpu.VMEM((D_MODEL, fc), BF16),
            pltpu.VMEM((D_MODEL, fc), BF16),
            pltpu.VMEM((fc, D_MODEL), BF16),
            pltpu.SemaphoreType.DMA((3,)),
        ],
    )
    return pl.pallas_call(
        _expert_body,
        grid_spec=grid_spec,
        out_shape=jax.ShapeDtypeStruct((n_slots, D_MODEL), F32),
        compiler_params=_cparams(("arbitrary", "arbitrary")),
        name="moe_experts",
    )(unit_e, unit_rows, unit_s0, tail, slot_tok3, xn, w_up, w_up, b_up[:, None, :], b_up[:, None, :],
      w_down, b_down[:, None, :])


def _combine_body(dest_ref, gate_ref, x2_ref, gf_ref, out_hbm, o_ref, buf, sem, *, tm):
    def row_copy(r, k, src):
        return pltpu.make_async_copy(out_hbm.at[pl.ds(src, 1)], buf.at[k, pl.ds(r, 1)], sem.at[0])

    def issue(r, c):
        for k in range(TOP_K):
            row_copy(r, k, dest_ref[0, 0, r * TOP_K + k]).start()
        return c

    lax.fori_loop(0, tm, issue, 0)

    def drain(r, c):
        for k in range(TOP_K):
            row_copy(0, k, 0).wait()
        return c

    lax.fori_loop(0, tm, drain, 0)
    gates = gate_ref[...]
    y = gates[:, 0:1] * buf[0]
    for k in range(1, TOP_K):
        y = y + gates[:, k:k + 1] * buf[k]
    o_ref[...] = _rms(x2_ref[...] + y, gf_ref[...])


def _combine_call(dest3, gates, x2, final_norm, expert_out):
    tm = COMBINE_ROWS
    T = x2.shape[0]
    row = pl.BlockSpec((tm, D_MODEL), lambda i: (i, 0))
    return pl.pallas_call(
        functools.partial(_combine_body, tm=tm),
        grid=(T // tm,),
        in_specs=[pl.BlockSpec((1, 1, tm * TOP_K), lambda i: (i, 0, 0), memory_space=pltpu.SMEM),
                  pl.BlockSpec((tm, LANES), lambda i: (i, 0)), row, _resident((1, D_MODEL)),
                  pl.BlockSpec(memory_space=pl.ANY)],
        out_specs=row,
        out_shape=jax.ShapeDtypeStruct((T, D_MODEL), F32),
        scratch_shapes=[pltpu.VMEM((TOP_K, tm, D_MODEL), F32), pltpu.SemaphoreType.DMA((1,))],
        compiler_params=_cparams(("arbitrary",)),
        name="moe_combine_norm",
    )(dest3, gates, x2, final_norm[None, :], expert_out)


def _routing_tables(idx, rank, counts, n_tokens):
    tk = n_tokens * TOP_K
    padded = (counts + MOE_ROW_TILE - 1) // MOE_ROW_TILE * MOE_ROW_TILE
    pad_end = jnp.cumsum(padded)
    pad_start = pad_end - padded
    dest = pad_start[idx] + rank
    n_slots = (tk // MOE_ROW_TILE + N_EXPERTS) * MOE_ROW_TILE
    tok_len = n_slots + MOE_UNIT_ROWS
    slot_tok = jnp.zeros((tok_len,), I32).at[dest.reshape(-1)].set(
        jnp.arange(tk, dtype=I32) // TOP_K)
    n_units = N_EXPERTS + tk // MOE_UNIT_ROWS
    units_per_e = (counts + MOE_UNIT_ROWS - 1) // MOE_UNIT_ROWS
    unit_end = jnp.cumsum(units_per_e)
    n_active = unit_end[-1]
    uid = jnp.arange(n_units, dtype=I32)
    ue = jnp.minimum(jnp.searchsorted(unit_end, uid, side='right'), N_EXPERTS - 1).astype(I32)
    k_in_e = uid - (unit_end - units_per_e)[ue]
    rows = jnp.clip(counts[ue] - k_in_e * MOE_UNIT_ROWS, 0, MOE_UNIT_ROWS)
    active = uid < n_active
    last_e = ue[jnp.maximum(n_active - 1, 0)]
    unit_e = jnp.where(active, ue, last_e).astype(I32)
    unit_rows = jnp.where(active, rows, 0).astype(I32)
    unit_s0 = jnp.where(active, pad_start[ue] + k_in_e * MOE_UNIT_ROWS, 0).astype(I32)
    used_tiles = pad_end[-1] // MOE_ROW_TILE
    tail = jnp.stack([used_tiles, n_slots // MOE_ROW_TILE - used_tiles]).astype(I32)
    return dest, slot_tok, unit_e, unit_rows, unit_s0, tail, n_slots, n_units


def kernel(x, mem, positions, attn_norm, w_in, q_a_norm, w_q_b, kv_a_norm, w_kv_b, ssm_lambda_re, ssm_lambda_im, ssm_log_dt, ssm_b_re, ssm_b_im, ssm_c_re, ssm_c_im, ssm_d, w_glu, b_glu, mix_norm_attn, mix_norm_ssm, w_out, xattn_norm, mem_norm, w_xq, w_xk, w_xv, w_xo, ffn_norm, w_router, b_router, w_up, b_up, w_down, b_down, final_norm):
    B, S, D = x.shape
    T = B * S
    assert w_in.shape[0] == 1, "single-layer block"
    l = 0
    xt = x.reshape(T, D)
    pos2d = positions.reshape(T, 1)
    mem2d = mem.reshape(B * MEM_LEN, D)
    q, k, v, u = _proj_call(xt, pos2d, attn_norm[l], w_in[l], q_a_norm[l], w_q_b[l],
                            kv_a_norm[l], w_kv_b[l])
    attn_o = _attn_call(q, k, v, B, S)
    ab_re, ab_im, bb_re, bb_im = _s5_param_call(ssm_lambda_re[l], ssm_lambda_im[l], ssm_log_dt[l],
                                                ssm_b_re[l], ssm_b_im[l])
    ssm_o = _s5_call(u, ab_re, ab_im, bb_re, bb_im, ssm_c_re[l], ssm_c_im[l], ssm_d[l],
                     w_glu[l], b_glu[l], B, S)
    x1, qx = _mix_call(xt, attn_o, ssm_o, mix_norm_attn[l], mix_norm_ssm[l], w_out[l],
                       xattn_norm[l], w_xq[l])
    kx, vx = _memkv_call(mem2d, mem_norm[l], w_xk[l], w_xv[l])
    x2, xn = _xattn_call(qx, kx, vx, x1, w_xo[l], ffn_norm[l], B, S)
    idx_w, gate_w, rank_w, counts_f = _router_call(xn, w_router[l], b_router[l])
    counts = counts_f[0].astype(I32)
    dest, slot_tok, unit_e, unit_rows, unit_s0, tail, n_slots, n_units = _routing_tables(
        idx_w[:, :TOP_K], rank_w[:, :TOP_K], counts, T)
    expert_out = _expert_call(unit_e, unit_rows, unit_s0, tail,
                              slot_tok.reshape(-1, 1, MOE_ROW_TILE), xn,
                              w_up[l], b_up[l], w_down[l], b_down[l], n_slots, n_units)
    out = _combine_call(dest.reshape(-1, 1, COMBINE_ROWS * TOP_K), gate_w, x2, final_norm, expert_out)
    return out.reshape(B, S, D)
```

```python
import functools
import math

import jax
import jax.numpy as jnp
from jax import lax
from jax.experimental import pallas as pl
from jax.experimental.pallas import tpu as pltpu

F32 = jnp.float32
BF16 = jnp.bfloat16
I32 = jnp.int32

D_MODEL = 2048
MEM_LEN = 256
MLA_HEADS = 8
QK_NOPE_DIM = 128
QK_ROPE_DIM = 64
V_HEAD_DIM = 128
Q_LORA_RANK = 384
KV_LORA_RANK = 256
ROPE_THETA = 10000.0
SSM_WIDTH = 1024
SSM_GROUP = 16
SSM_GROUPS = 64
SSM_STATE = 64
XATTN_HEADS = 4
XATTN_HEAD_DIM = D_MODEL // XATTN_HEADS
N_EXPERTS = 32
TOP_K = 4
D_FF = D_MODEL
SWIGLU_LIMIT = 7.0
SWIGLU_ALPHA = 1.702
EPS = 1e-6
COL_Q = Q_LORA_RANK
COL_KV = COL_Q + KV_LORA_RANK
COL_KPE = COL_KV + QK_ROPE_DIM

LANES = 128
SUBLANES = 8
MXU_DIM = 256
VMEM_LIMIT_BYTES = 56 * 1024 * 1024

HEAD_SLOT = 2 * LANES
ATTN_HEADS_PER_STEP = 2
STATE_ELEMS = SSM_GROUPS * SSM_STATE
STATE_SLABS = STATE_ELEMS // (SUBLANES * LANES)

MOE_ROW_TILE = 256
MOE_UNIT_ROWS = 1536
MOE_FF_CHUNK = 512
MOE_NJ = D_FF // MOE_FF_CHUNK
COMBINE_ROWS = 256


def _cparams(semantics):
    return pltpu.CompilerParams(dimension_semantics=semantics,
                                vmem_limit_bytes=VMEM_LIMIT_BYTES)


def _resident(shape):
    nd = len(shape)
    return pl.BlockSpec(shape, lambda *_: (0,) * nd, pipeline_mode=pl.Buffered(1))


def _rms(xf, g):
    ms = jnp.mean(xf * xf, axis=-1, keepdims=True)
    return xf * lax.rsqrt(ms + EPS) * g


def _proj_body(x_ref, pos_ref, g_ref, w1_ref, qn_ref, wqb_ref, kvn_ref, wkvb_ref, freq_ref,
               q_ref, k_ref, v_ref, u_ref, *, scale):
    h = _rms(x_ref[...], g_ref[...]).astype(BF16)
    p = jnp.dot(h, w1_ref[...], preferred_element_type=F32)
    o_kv, o_ka, o_kb, o_u = COL_Q, COL_KV, COL_KV + LANES, COL_KV + 2 * LANES
    cq = p[:, :o_kv]
    ckv = p[:, o_kv:o_ka]
    kpa = p[:, o_ka:o_kb]
    kpb = p[:, o_kb:o_u]
    u_ref[...] = p[:, o_u:]
    ang = pos_ref[...].astype(F32) * freq_ref[...]
    c = jnp.cos(ang)
    s = jnp.sin(ang)
    qq = jnp.dot(_rms(cq, qn_ref[...]).astype(BF16), wqb_ref[...], preferred_element_type=F32)
    kv = jnp.dot(_rms(ckv, kvn_ref[...]).astype(BF16), wkvb_ref[...], preferred_element_type=F32)
    kpe = (kpa * c + kpb * s).astype(BF16)
    swap0 = MLA_HEADS * HEAD_SLOT
    for hd in range(MLA_HEADS):
        a0 = hd * HEAD_SLOT
        q_ref[:, a0:a0 + LANES] = (qq[:, a0:a0 + LANES] * scale).astype(BF16)
        qa = qq[:, a0 + LANES:a0 + HEAD_SLOT]
        qb = qq[:, swap0 + hd * LANES:swap0 + (hd + 1) * LANES]
        q_ref[:, a0 + LANES:a0 + HEAD_SLOT] = ((qa * c + qb * s) * scale).astype(BF16)
        k_ref[:, a0:a0 + LANES] = kv[:, hd * LANES:(hd + 1) * LANES].astype(BF16)
        k_ref[:, a0 + LANES:a0 + HEAD_SLOT] = kpe
    v_ref[...] = kv[:, MLA_HEADS * QK_NOPE_DIM:].astype(BF16)


def _rope_slot_pair(w):
    half = QK_ROPE_DIM // 2
    pad = jnp.zeros(w.shape[:-1] + (LANES - QK_ROPE_DIM,), w.dtype)
    a = jnp.concatenate([w, pad], axis=-1)
    b = jnp.concatenate([-w[..., half:], w[..., :half], pad], axis=-1)
    return a, b


def _proj_call(x2d, pos2d, attn_norm, w_in, q_a_norm, w_q_b, kv_a_norm, w_kv_b, tm=256):
    T = x2d.shape[0]
    kpa, kpb = _rope_slot_pair(w_in[:, COL_KV:COL_KPE])
    w1 = jnp.concatenate([w_in[:, :COL_KV], kpa, kpb, w_in[:, COL_KPE:]], axis=1).astype(BF16)
    wq = w_q_b.reshape(Q_LORA_RANK, MLA_HEADS, QK_NOPE_DIM + QK_ROPE_DIM)
    qpa, qpb = _rope_slot_pair(wq[..., QK_NOPE_DIM:])
    wqb = jnp.concatenate([
        jnp.concatenate([wq[..., :QK_NOPE_DIM], qpa], axis=-1).reshape(Q_LORA_RANK, MLA_HEADS * HEAD_SLOT),
        qpb.reshape(Q_LORA_RANK, MLA_HEADS * LANES)], axis=1).astype(BF16)
    wkv = w_kv_b.reshape(KV_LORA_RANK, MLA_HEADS, QK_NOPE_DIM + V_HEAD_DIM)
    wkvb = jnp.concatenate([wkv[..., :QK_NOPE_DIM].reshape(KV_LORA_RANK, -1),
                            wkv[..., QK_NOPE_DIM:].reshape(KV_LORA_RANK, -1)], axis=1).astype(BF16)
    inv_freq = 1.0 / (ROPE_THETA ** (jnp.arange(0, QK_ROPE_DIM, 2, dtype=F32) / QK_ROPE_DIM))
    freq = jnp.concatenate([inv_freq, inv_freq, jnp.zeros((LANES - QK_ROPE_DIM,), F32)])[None, :]
    scale = (QK_NOPE_DIM + QK_ROPE_DIM) ** -0.5
    n1 = w1.shape[1]
    row = lambda w: pl.BlockSpec((tm, w), lambda i: (i, 0))
    return pl.pallas_call(
        functools.partial(_proj_body, scale=scale),
        grid=(T // tm,),
        in_specs=[row(D_MODEL), row(1), _resident((1, D_MODEL)), _resident((D_MODEL, n1)),
                  _resident((1, Q_LORA_RANK)), _resident(wqb.shape),
                  _resident((1, KV_LORA_RANK)), _resident(wkvb.shape), _resident((1, LANES))],
        out_specs=[row(MLA_HEADS * HEAD_SLOT), row(MLA_HEADS * HEAD_SLOT),
                   row(MLA_HEADS * V_HEAD_DIM), row(SSM_WIDTH)],
        out_shape=[jax.ShapeDtypeStruct((T, MLA_HEADS * HEAD_SLOT), BF16),
                   jax.ShapeDtypeStruct((T, MLA_HEADS * HEAD_SLOT), BF16),
                   jax.ShapeDtypeStruct((T, MLA_HEADS * V_HEAD_DIM), BF16),
                   jax.ShapeDtypeStruct((T, SSM_WIDTH), F32)],
        compiler_params=_cparams(("parallel",)),
        name="proj_mla_rope",
    )(x2d, pos2d, attn_norm[None, :], w1, q_a_norm[None, :], wqb, kv_a_norm[None, :], wkvb, freq)


def _attn_body(q_ref, k_ref, v_ref, o_ref, *, tq):
    i = pl.program_id(2)

    def step(j, carry, diagonal):
        off = pl.multiple_of(j * tq, tq)
        out = []
        for hd in range(ATTN_HEADS_PER_STEP):
            m, l, acc = carry[hd]
            q = q_ref[:, hd * HEAD_SLOT:(hd + 1) * HEAD_SLOT]
            kj = k_ref[pl.ds(off, tq), hd * HEAD_SLOT:(hd + 1) * HEAD_SLOT]
            vj = v_ref[pl.ds(off, tq), hd * V_HEAD_DIM:(hd + 1) * V_HEAD_DIM]
            s = lax.dot_general(q, kj, (((1,), (1,)), ((), ())), preferred_element_type=F32)
            if diagonal:
                rows = lax.broadcasted_iota(I32, s.shape, 0)
                cols = lax.broadcasted_iota(I32, s.shape, 1)
                s = jnp.where(rows >= cols, s, -jnp.inf)
            m_new = jnp.maximum(m, jnp.max(s, axis=-1, keepdims=True))
            alpha = jnp.exp(m - m_new)
            p = jnp.exp(s - m_new)
            l = alpha * l + jnp.sum(p, axis=-1, keepdims=True)
            acc = alpha * acc + jnp.dot(p.astype(BF16), vj, preferred_element_type=F32)
            out.append((m_new, l, acc))
        return tuple(out)

    init = tuple((jnp.full((tq, 1), -jnp.inf, F32), jnp.zeros((tq, 1), F32),
                  jnp.zeros((tq, V_HEAD_DIM), F32)) for _ in range(ATTN_HEADS_PER_STEP))
    carry = lax.fori_loop(0, i, lambda j, c: step(j, c, False), init)
    carry = step(i, carry, True)
    for hd in range(ATTN_HEADS_PER_STEP):
        _, l, acc = carry[hd]
        o_ref[:, hd * V_HEAD_DIM:(hd + 1) * V_HEAD_DIM] = acc / l


def _attn_call(q, k, v, batch, seq, tq=512):
    nq = seq // tq
    hp = ATTN_HEADS_PER_STEP
    return pl.pallas_call(
        functools.partial(_attn_body, tq=tq),
        grid=(batch, MLA_HEADS // hp, nq),
        in_specs=[pl.BlockSpec((tq, hp * HEAD_SLOT), lambda b, h, i: (b * nq + i, h)),
                  pl.BlockSpec((seq, hp * HEAD_SLOT), lambda b, h, i: (b, h)),
                  pl.BlockSpec((seq, hp * V_HEAD_DIM), lambda b, h, i: (b, h))],
        out_specs=pl.BlockSpec((tq, hp * V_HEAD_DIM), lambda b, h, i: (b * nq + i, h)),
        out_shape=jax.ShapeDtypeStruct((batch * seq, MLA_HEADS * V_HEAD_DIM), F32),
        compiler_params=_cparams(("parallel", "parallel", "arbitrary")),
        name="mla_attention",
    )(q, k, v)


def _s5_param_body(lr_ref, li_ref, ldt_ref, br_ref, bi_ref, abr_ref, abi_ref, bbr_ref, bbi_ref):
    lr = lr_ref[...]
    li = li_ref[...]
    dt = jnp.exp(ldt_ref[...])
    mag = jnp.exp(lr * dt)
    ab_re = mag * jnp.cos(li * dt)
    ab_im = mag * jnp.sin(li * dt)
    nr, ni = ab_re - 1.0, ab_im
    den = lr * lr + li * li
    coef_re = (nr * lr + ni * li) / den
    coef_im = (ni * lr - nr * li) / den
    abr_ref[...] = ab_re
    abi_ref[...] = ab_im
    for c in range(SSM_GROUP):
        br = br_ref[c]
        bi = bi_ref[c]
        bbr_ref[c] = coef_re * br - coef_im * bi
        bbi_ref[c] = coef_re * bi + coef_im * br


def _s5_param_call(lam_re, lam_im, log_dt, b_re, b_im):
    G, P, C = SSM_GROUPS, SSM_STATE, SSM_GROUP
    gp = jax.ShapeDtypeStruct((G, P), F32)
    cgp = jax.ShapeDtypeStruct((C, G, P), F32)
    return pl.pallas_call(
        _s5_param_body, out_shape=[gp, gp, cgp, cgp], name="s5_discretise",
    )(lam_re, lam_im, log_dt[:, None], b_re.transpose(2, 0, 1), b_im.transpose(2, 0, 1))


def _s5_body(u_ref, bre_ref, bim_ref, cre_ref, cim_ref, ar_ref, ai_ref, d_ref, wg_ref, bg_ref,
             o_ref, sre, sim, st_re, st_im, *, tt):
    nk = SSM_WIDTH // MXU_DIM
    cols_per_k = STATE_ELEMS // nk
    lane_blocks = cols_per_k // LANES
    c_idx = pl.program_id(1)

    @pl.when(c_idx == 0)
    def _():
        st_re[...] = jnp.zeros_like(st_re)
        st_im[...] = jnp.zeros_like(st_im)

    u = u_ref[...]
    ub = u.astype(BF16)
    for k in range(nk):
        uk = ub[:, k * MXU_DIM:(k + 1) * MXU_DIM]
        pr = jnp.dot(uk, bre_ref[k], preferred_element_type=F32)
        pi = jnp.dot(uk, bim_ref[k], preferred_element_type=F32)
        for m in range(lane_blocks):
            blk = k * lane_blocks + m
            s_, j_ = blk // STATE_SLABS, blk % STATE_SLABS
            sre[j_, pl.ds(s_, tt, stride=SUBLANES), :] = pr[:, m * LANES:(m + 1) * LANES]
            sim[j_, pl.ds(s_, tt, stride=SUBLANES), :] = pi[:, m * LANES:(m + 1) * LANES]

    ar = [ar_ref[j] for j in range(STATE_SLABS)]
    ai = [ai_ref[j] for j in range(STATE_SLABS)]

    def scan_step(t, carry):
        xr, xi = carry
        row = pl.multiple_of(t * SUBLANES, SUBLANES)
        nxr, nxi = [], []
        for j in range(STATE_SLABS):
            br = sre[j, pl.ds(row, SUBLANES), :]
            bi = sim[j, pl.ds(row, SUBLANES), :]
            r = ar[j] * xr[j] - ai[j] * xi[j] + br
            im = ar[j] * xi[j] + ai[j] * xr[j] + bi
            sre[j, pl.ds(row, SUBLANES), :] = r
            sim[j, pl.ds(row, SUBLANES), :] = im
            nxr.append(r)
            nxi.append(im)
        return tuple(nxr), tuple(nxi)

    x0 = (tuple(st_re[j] for j in range(STATE_SLABS)), tuple(st_im[j] for j in range(STATE_SLABS)))
    xr, xi = lax.fori_loop(0, tt, scan_step, x0, unroll=8)
    for j in range(STATE_SLABS):
        st_re[j] = xr[j]
        st_im[j] = xi[j]

    ys = []
    for k in range(nk):
        xre, xim = [], []
        for m in range(lane_blocks):
            blk = k * lane_blocks + m
            s_, j_ = blk // STATE_SLABS, blk % STATE_SLABS
            xre.append(sre[j_, pl.ds(s_, tt, stride=SUBLANES), :])
            xim.append(sim[j_, pl.ds(s_, tt, stride=SUBLANES), :])
        xre = jnp.concatenate(xre, axis=1).astype(BF16)
        xim = jnp.concatenate(xim, axis=1).astype(BF16)
        ys.append(jnp.dot(xre, cre_ref[k], preferred_element_type=F32)
                  - jnp.dot(xim, cim_ref[k], preferred_element_type=F32))
    y = jnp.concatenate(ys, axis=1) + d_ref[...] * u
    y = jax.nn.gelu(y)
    gate = jax.nn.sigmoid(jnp.dot(y.astype(BF16), wg_ref[...], preferred_element_type=F32) + bg_ref[...])
    o_ref[...] = y * gate


def _block_diag(blocks):
    nk, ng, r, c = blocks.shape
    eye = jnp.eye(ng, dtype=blocks.dtype)
    out = blocks[:, :, :, None, :] * eye[None, :, None, :, None]
    return out.reshape(nk, ng * r, ng * c)


def _s5_call(u, ab_re, ab_im, bb_re, bb_im, c_re, c_im, d_skip, w_glu, b_glu, batch, seq, tt=256):
    G, P, C = SSM_GROUPS, SSM_STATE, SSM_GROUP
    nk = SSM_WIDTH // MXU_DIM
    gk = G // nk
    def bmat(bb):
        blocks = bb.transpose(1, 0, 2).reshape(nk, gk, C, P)
        return _block_diag(blocks).astype(BF16)
    def cmat(cc):
        blocks = cc.transpose(0, 2, 1).reshape(nk, gk, P, C)
        return _block_diag(blocks).astype(BF16)
    def slab(a):
        return a.reshape(SUBLANES, STATE_SLABS, LANES).transpose(1, 0, 2)
    nc = seq // tt
    row = pl.BlockSpec((tt, SSM_WIDTH), lambda b, c: (b * nc + c, 0))
    return pl.pallas_call(
        functools.partial(_s5_body, tt=tt),
        grid=(batch, nc),
        in_specs=[row, _resident((nk, MXU_DIM, STATE_ELEMS // nk)), _resident((nk, MXU_DIM, STATE_ELEMS // nk)),
                  _resident((nk, STATE_ELEMS // nk, MXU_DIM)), _resident((nk, STATE_ELEMS // nk, MXU_DIM)),
                  _resident((STATE_SLABS, SUBLANES, LANES)), _resident((STATE_SLABS, SUBLANES, LANES)),
                  _resident((1, SSM_WIDTH)), _resident((SSM_WIDTH, SSM_WIDTH)), _resident((1, SSM_WIDTH))],
        out_specs=row,
        out_shape=jax.ShapeDtypeStruct((batch * seq, SSM_WIDTH), F32),
        scratch_shapes=[pltpu.VMEM((STATE_SLABS, tt * SUBLANES, LANES), F32),
                        pltpu.VMEM((STATE_SLABS, tt * SUBLANES, LANES), F32),
                        pltpu.VMEM((STATE_SLABS, SUBLANES, LANES), F32),
                        pltpu.VMEM((STATE_SLABS, SUBLANES, LANES), F32)],
        compiler_params=_cparams(("parallel", "arbitrary")),
        name="s5_scan_glu",
    )(u, bmat(bb_re), bmat(bb_im), cmat(c_re), cmat(c_im), slab(ab_re), slab(ab_im),
      d_skip[None, :], w_glu.astype(BF16), b_glu[None, :])


def _mix_body(x_ref, ao_ref, so_ref, ga_ref, gs_ref, woa_ref, wos_ref, gx_ref, wxq_ref,
              x1_ref, qx_ref, *, xscale):
    a = _rms(ao_ref[...], ga_ref[...]).astype(BF16)
    s = _rms(so_ref[...], gs_ref[...]).astype(BF16)
    x1 = (x_ref[...] + jnp.dot(a, woa_ref[...], preferred_element_type=F32)
          + jnp.dot(s, wos_ref[...], preferred_element_type=F32))
    x1_ref[...] = x1
    hq = _rms(x1, gx_ref[...]).astype(BF16)
    qx_ref[...] = (jnp.dot(hq, wxq_ref[...], preferred_element_type=F32) * xscale).astype(BF16)


def _mix_call(x2d, attn_o, ssm_o, g_attn, g_ssm, w_out, g_x, w_xq, tm=256):
    T = x2d.shape[0]
    na = MLA_HEADS * V_HEAD_DIM
    row = lambda w: pl.BlockSpec((tm, w), lambda i: (i, 0))
    return pl.pallas_call(
        functools.partial(_mix_body, xscale=XATTN_HEAD_DIM ** -0.5),
        grid=(T // tm,),
        in_specs=[row(D_MODEL), row(na), row(SSM_WIDTH), _resident((1, na)), _resident((1, SSM_WIDTH)),
                  _resident((na, D_MODEL)), _resident((SSM_WIDTH, D_MODEL)),
                  _resident((1, D_MODEL)), _resident((D_MODEL, D_MODEL))],
        out_specs=[row(D_MODEL), row(D_MODEL)],
        out_shape=[jax.ShapeDtypeStruct((T, D_MODEL), F32), jax.ShapeDtypeStruct((T, D_MODEL), BF16)],
        compiler_params=_cparams(("parallel",)),
        name="mix_out_proj",
    )(x2d, attn_o, ssm_o, g_attn[None, :], g_ssm[None, :], w_out[:na].astype(BF16),
      w_out[na:].astype(BF16), g_x[None, :], w_xq.astype(BF16))


def _memkv_body(m_ref, g_ref, wk_ref, wv_ref, k_ref, v_ref):
    h = _rms(m_ref[...], g_ref[...]).astype(BF16)
    k_ref[...] = jnp.dot(h, wk_ref[...], preferred_element_type=F32).astype(BF16)
    v_ref[...] = jnp.dot(h, wv_ref[...], preferred_element_type=F32).astype(BF16)


def _memkv_call(mem2d, g_mem, w_xk, w_xv, tm=256):
    R = mem2d.shape[0]
    row = pl.BlockSpec((tm, D_MODEL), lambda i: (i, 0))
    return pl.pallas_call(
        _memkv_body,
        grid=(R // tm,),
        in_specs=[row, _resident((1, D_MODEL)), _resident((D_MODEL, D_MODEL)), _resident((D_MODEL, D_MODEL))],
        out_specs=[row, row],
        out_shape=[jax.ShapeDtypeStruct((R, D_MODEL), BF16)] * 2,
        compiler_params=_cparams(("parallel",)),
        name="mem_kv_proj",
    )(mem2d, g_mem[None, :], w_xk.astype(BF16), w_xv.astype(BF16))


def _pack_bf16_pairs(xf):
    c = xf.shape[1] // 2
    bits = pltpu.bitcast(xf.astype(BF16).astype(F32), jnp.uint32)
    return lax.shift_right_logical(bits[:, :c], jnp.uint32(16)) | bits[:, c:]


def _unpack_bf16_pairs(w):
    lo = pltpu.bitcast(lax.shift_left(w, jnp.uint32(16)), F32).astype(BF16)
    hi = pltpu.bitcast(w & jnp.uint32(0xFFFF0000), F32).astype(BF16)
    return lo, hi


def _xattn_body(q_ref, k_ref, v_ref, x1_ref, wo_ref, gf_ref, x2_ref, xn_ref, xp_ref):
    outs = []
    for h in range(XATTN_HEADS):
        sl = slice(h * XATTN_HEAD_DIM, (h + 1) * XATTN_HEAD_DIM)
        s = lax.dot_general(q_ref[:, sl], k_ref[:, sl], (((1,), (1,)), ((), ())),
                            preferred_element_type=F32)
        p = jnp.exp(s - jnp.max(s, axis=-1, keepdims=True))
        denom = jnp.sum(p, axis=-1, keepdims=True)
        o = jnp.dot(p.astype(BF16), v_ref[:, sl], preferred_element_type=F32) / denom
        outs.append(o.astype(BF16))
    o = jnp.concatenate(outs, axis=1)
    x2 = x1_ref[...] + jnp.dot(o, wo_ref[...], preferred_element_type=F32)
    x2_ref[...] = x2
    xn = _rms(x2, gf_ref[...])
    xn_ref[...] = xn
    xp_ref[...] = _pack_bf16_pairs(xn)


def _xattn_call(qx, kx, vx, x1, w_xo, g_ffn, batch, seq, tq=256):
    nq = seq // tq
    row = pl.BlockSpec((tq, D_MODEL), lambda b, i: (b * nq + i, 0))
    half = pl.BlockSpec((tq, D_MODEL // 2), lambda b, i: (b * nq + i, 0))
    memb = pl.BlockSpec((MEM_LEN, D_MODEL), lambda b, i: (b, 0))
    return pl.pallas_call(
        _xattn_body,
        grid=(batch, nq),
        in_specs=[row, memb, memb, row, _resident((D_MODEL, D_MODEL)), _resident((1, D_MODEL))],
        out_specs=[row, row, half],
        out_shape=[jax.ShapeDtypeStruct((batch * seq, D_MODEL), F32),
                   jax.ShapeDtypeStruct((batch * seq, D_MODEL), F32),
                   jax.ShapeDtypeStruct((batch * seq, D_MODEL // 2), jnp.uint32)],
        compiler_params=_cparams(("parallel", "parallel")),
        name="cross_attention",
    )(qx, kx, vx, x1, w_xo.astype(BF16), g_ffn[None, :])


def _router_body(xn_ref, wr_ref, br_ref, idx_ref, gate_ref, rank_ref, cnt_ref, carry, *, tm):
    @pl.when(pl.program_id(0) == 0)
    def _():
        carry[...] = jnp.zeros_like(carry)

    logits = jnp.dot(xn_ref[...], wr_ref[...], preferred_element_type=F32,
                     precision=lax.Precision.HIGHEST) + br_ref[...]
    lane = lax.broadcasted_iota(I32, logits.shape, 1)
    work = logits
    sels, vals, ids = [], [], []
    for _ in range(TOP_K):
        mx = jnp.max(work, axis=-1, keepdims=True)
        am = jnp.min(jnp.where(work == mx, lane, N_EXPERTS), axis=-1, keepdims=True)
        sel = lane == am
        sels.append(sel)
        vals.append(mx)
        ids.append(am)
        work = jnp.where(sel, -jnp.inf, work)
    es = [jnp.exp(v - vals[0]) for v in vals]
    denom = es[0] + es[1] + es[2] + es[3]
    member = jnp.zeros(logits.shape, F32)
    for sel in sels:
        member = member + jnp.where(sel, 1.0, 0.0)
    r_i = lax.broadcasted_iota(I32, (tm, tm), 0)
    c_i = lax.broadcasted_iota(I32, (tm, tm), 1)
    tri = jnp.where(r_i > c_i, 1.0, 0.0).astype(BF16)
    before = jnp.dot(tri, member.astype(BF16), preferred_element_type=F32) + carry[...]
    out_lane = lax.broadcasted_iota(I32, (tm, LANES), 1)
    idx_o = jnp.zeros((tm, LANES), I32)
    gate_o = jnp.zeros((tm, LANES), F32)
    rank_o = jnp.zeros((tm, LANES), I32)
    for k in range(TOP_K):
        rk = jnp.sum(jnp.where(sels[k], before, 0.0), axis=-1, keepdims=True).astype(I32)
        idx_o = jnp.where(out_lane == k, ids[k], idx_o)
        gate_o = jnp.where(out_lane == k, es[k] / denom, gate_o)
        rank_o = jnp.where(out_lane == k, rk, rank_o)
    idx_ref[...] = idx_o
    gate_ref[...] = gate_o
    rank_ref[...] = rank_o
    carry[...] = carry[...] + jnp.sum(member, axis=0, keepdims=True)
    cnt_ref[...] = carry[...]


def _router_call(xn, w_router, b_router, tm=256):
    T = xn.shape[0]
    wide = pl.BlockSpec((tm, LANES), lambda i: (i, 0))
    return pl.pallas_call(
        functools.partial(_router_body, tm=tm),
        grid=(T // tm,),
        in_specs=[pl.BlockSpec((tm, D_MODEL), lambda i: (i, 0)),
                  _resident((D_MODEL, N_EXPERTS)), _resident((1, N_EXPERTS))],
        out_specs=[wide, wide, wide, pl.BlockSpec((1, N_EXPERTS), lambda i: (0, 0))],
        out_shape=[jax.ShapeDtypeStruct((T, LANES), I32), jax.ShapeDtypeStruct((T, LANES), F32),
                   jax.ShapeDtypeStruct((T, LANES), I32), jax.ShapeDtypeStruct((1, N_EXPERTS), F32)],
        scratch_shapes=[pltpu.VMEM((1, N_EXPERTS), F32)],
        compiler_params=_cparams(("arbitrary",)),
        name="moe_router",
    )(xn, w_router, b_router[None, :])


def _dispatch_body(fill_ref, dest_ref, xp_ref, xs_hbm, zbuf, sems, *, tm):
    @pl.when(pl.program_id(0) == 0)
    def _():
        zbuf[...] = jnp.zeros_like(zbuf)

        def zero_copy(e):
            dst = pl.multiple_of(fill_ref[e] * MOE_ROW_TILE, MOE_ROW_TILE)
            return pltpu.make_async_copy(zbuf, xs_hbm.at[pl.ds(dst, MOE_ROW_TILE)], sems.at[0])

        def start(e, c):
            @pl.when(fill_ref[e] >= 0)
            def _():
                zero_copy(e).start()
            return c

        def wait(e, c):
            @pl.when(fill_ref[e] >= 0)
            def _():
                zero_copy(e).wait()
            return c

        lax.fori_loop(0, 2 * N_EXPERTS, start, 0)
        lax.fori_loop(0, 2 * N_EXPERTS, wait, 0)

    def row_copy(r, dst):
        return pltpu.make_async_copy(xp_ref.at[pl.ds(r, 1)], xs_hbm.at[pl.ds(dst, 1)], sems.at[1])

    def issue(r, c):
        for k in range(TOP_K):
            row_copy(r, dest_ref[0, 0, r * TOP_K + k]).start()
        return c

    lax.fori_loop(0, tm, issue, 0)

    def drain(r, c):
        for k in range(TOP_K):
            row_copy(0, 0).wait()
        return c

    lax.fori_loop(0, tm, drain, 0)


def _dispatch_call(fill_tile, dest3, xnp, n_slots):
    tm = COMBINE_ROWS
    T, half = xnp.shape
    grid_spec = pltpu.PrefetchScalarGridSpec(
        num_scalar_prefetch=1,
        grid=(T // tm,),
        in_specs=[pl.BlockSpec((1, 1, tm * TOP_K), lambda i, fl: (i, 0, 0), memory_space=pltpu.SMEM),
                  pl.BlockSpec((tm, half), lambda i, fl: (i, 0))],
        out_specs=pl.BlockSpec(memory_space=pl.ANY),
        scratch_shapes=[pltpu.VMEM((MOE_ROW_TILE, half), jnp.uint32), pltpu.SemaphoreType.DMA((2,))],
    )
    return pl.pallas_call(
        functools.partial(_dispatch_body, tm=tm),
        grid_spec=grid_spec,
        out_shape=jax.ShapeDtypeStruct((n_slots, half), jnp.uint32),
        compiler_params=_cparams(("arbitrary",)),
        name="moe_dispatch",
    )(fill_tile, dest3, xnp)


def _expert_body(ue_ref, ur_ref, us_ref, tail_ref, xs_hbm, wg_ref, wl_ref, bg_ref, bl_ref,
                 wd_ref, bd_ref, out_hbm, raw, acc, wg_bf, wl_bf, wd_bf, sems):
    u = pl.program_id(0)
    j = pl.program_id(1)
    rows = ur_ref[u]
    s0 = us_ref[u]
    nsub = (rows + (MOE_ROW_TILE - 1)) // MOE_ROW_TILE
    half = D_MODEL // 2

    @pl.when(jnp.logical_and(j == 0, rows > 0))
    def _():
        def in_copy(r):
            off = pl.multiple_of(r * MOE_ROW_TILE, MOE_ROW_TILE)
            src = pl.multiple_of(s0 + r * MOE_ROW_TILE, MOE_ROW_TILE)
            return pltpu.make_async_copy(xs_hbm.at[pl.ds(src, MOE_ROW_TILE)],
                                         raw.at[pl.ds(off, MOE_ROW_TILE)], sems.at[0])

        def start(r, c):
            in_copy(r).start()
            return c

        lax.fori_loop(0, nsub, start, 0)
        bd = jnp.broadcast_to(bd_ref[0], (MOE_ROW_TILE, D_MODEL))

        def init(r, c):
            off = pl.multiple_of(r * MOE_ROW_TILE, MOE_ROW_TILE)
            acc[pl.ds(off, MOE_ROW_TILE), :] = bd
            return c

        def wait(r, c):
            in_copy(r).wait()
            return c

        lax.fori_loop(0, nsub, init, 0)
        lax.fori_loop(0, nsub, wait, 0)

    bg = bg_ref[0]
    bl = bl_ref[0]

    def up(off, wg, wl):
        lo, hi = _unpack_bf16_pairs(raw[pl.ds(off, MOE_ROW_TILE), :])
        g = (jnp.dot(lo, wg[:half], preferred_element_type=F32)
             + jnp.dot(hi, wg[half:], preferred_element_type=F32) + bg)
        l = (jnp.dot(lo, wl[:half], preferred_element_type=F32)
             + jnp.dot(hi, wl[half:], preferred_element_type=F32) + bl)
        glu = jnp.minimum(g, SWIGLU_LIMIT)
        lin = jnp.clip(l, -SWIGLU_LIMIT, SWIGLU_LIMIT)
        return (glu * jax.nn.sigmoid(SWIGLU_ALPHA * glu) * (lin + 1.0)).astype(BF16)

    def down(off, act, wd):
        acc[pl.ds(off, MOE_ROW_TILE), :] += jnp.dot(act, wd, preferred_element_type=F32)

    @pl.when(nsub > 0)
    def _():
        wg = wg_ref[0].astype(BF16)
        wl = wl_ref[0].astype(BF16)
        wd = wd_ref[0].astype(BF16)
        wg_bf[...] = wg
        wl_bf[...] = wl
        wd_bf[...] = wd
        act0 = up(0, wg, wl)

        def body(r, act):
            off = pl.multiple_of(r * MOE_ROW_TILE, MOE_ROW_TILE)
            nxt = pl.multiple_of(off + MOE_ROW_TILE, MOE_ROW_TILE)
            down(off, act, wd_bf[...])
            return up(nxt, wg_bf[...], wl_bf[...])

        act_last = lax.fori_loop(0, nsub - 1, body, act0)
        down(pl.multiple_of((nsub - 1) * MOE_ROW_TILE, MOE_ROW_TILE), act_last, wd_bf[...])

    @pl.when(j == MOE_NJ - 1)
    def _():
        def out_copy(r):
            off = pl.multiple_of(r * MOE_ROW_TILE, MOE_ROW_TILE)
            dst = pl.multiple_of(s0 + r * MOE_ROW_TILE, MOE_ROW_TILE)
            return pltpu.make_async_copy(acc.at[pl.ds(off, MOE_ROW_TILE)],
                                         out_hbm.at[pl.ds(dst, MOE_ROW_TILE)], sems.at[2])

        def start(r, c):
            out_copy(r).start()
            return c

        def wait(r, c):
            out_copy(r).wait()
            return c

        lax.fori_loop(0, nsub, start, 0)
        lax.fori_loop(0, nsub, wait, 0)

    @pl.when(jnp.logical_and(u == pl.num_programs(0) - 1, j == MOE_NJ - 1))
    def _():
        acc[pl.ds(0, MOE_ROW_TILE), :] = jnp.zeros((MOE_ROW_TILE, D_MODEL), F32)
        first = tail_ref[0]
        count = tail_ref[1]

        def zero_copy(t):
            dst = pl.multiple_of((first + t) * MOE_ROW_TILE, MOE_ROW_TILE)
            return pltpu.make_async_copy(acc.at[pl.ds(0, MOE_ROW_TILE)],
                                         out_hbm.at[pl.ds(dst, MOE_ROW_TILE)], sems.at[2])

        def start(t, c):
            zero_copy(t).start()
            return c

        def wait(t, c):
            zero_copy(t).wait()
            return c

        lax.fori_loop(0, count, start, 0)
        lax.fori_loop(0, count, wait, 0)


def _expert_call(unit_e, unit_rows, unit_s0, tail, xs, w_up, b_up, w_down, b_down, n_slots, n_units):
    fc = MOE_FF_CHUNK
    grid_spec = pltpu.PrefetchScalarGridSpec(
        num_scalar_prefetch=4,
        grid=(n_units, MOE_NJ),
        in_specs=[
            pl.BlockSpec(memory_space=pl.ANY),
            pl.BlockSpec((1, D_MODEL, fc), lambda u, j, ue, ur, us, tl: (ue[u], 0, j)),
            pl.BlockSpec((1, D_MODEL, fc), lambda u, j, ue, ur, us, tl: (ue[u], 0, MOE_NJ + j)),
            pl.BlockSpec((1, 1, fc), lambda u, j, ue, ur, us, tl: (ue[u], 0, j)),
            pl.BlockSpec((1, 1, fc), lambda u, j, ue, ur, us, tl: (ue[u], 0, MOE_NJ + j)),
            pl.BlockSpec((1, fc, D_MODEL), lambda u, j, ue, ur, us, tl: (ue[u], j, 0)),
            pl.BlockSpec((1, 1, D_MODEL), lambda u, j, ue, ur, us, tl: (ue[u], 0, 0)),
        ],
        out_specs=pl.BlockSpec(memory_space=pl.ANY),
        scratch_shapes=[
            pltpu.VMEM((MOE_UNIT_ROWS, D_MODEL // 2), jnp.uint32),
            pltpu.VMEM((MOE_UNIT_ROWS, D_MODEL), F32),
            pltpu.VMEM((D_MODEL, fc), BF16),
            pltpu.VMEM((D_MODEL, fc), BF16),
            pltpu.VMEM((fc, D_MODEL), BF16),
            pltpu.SemaphoreType.DMA((3,)),
        ],
    )
    return pl.pallas_call(
        _expert_body,
        grid_spec=grid_spec,
        out_shape=jax.ShapeDtypeStruct((n_slots, D_MODEL), F32),
        compiler_params=_cparams(("arbitrary", "arbitrary")),
        name="moe_experts",
    )(unit_e, unit_rows, unit_s0, tail, xs, w_up, w_up, b_up[:, None, :], b_up[:, None, :],
      w_down, b_down[:, None, :])


def _combine_body(dest_ref, next_ref, gate_ref, x2_ref, gf_ref, out_hbm, o_ref, buf, sems, *, tm):
    i = pl.program_id(0)
    n = pl.num_programs(0)
    slot = lax.rem(i, 2)

    def row_copy(sl, r, k, src):
        return pltpu.make_async_copy(out_hbm.at[pl.ds(src, 1)], buf.at[sl, k, pl.ds(r, 1)], sems.at[sl])

    def issue_tile(table, sl):
        def issue(r, c):
            for k in range(TOP_K):
                row_copy(sl, r, k, table[0, 0, r * TOP_K + k]).start()
            return c

        lax.fori_loop(0, tm, issue, 0)

    @pl.when(i == 0)
    def _():
        issue_tile(dest_ref, 0)

    @pl.when(i + 1 < n)
    def _():
        issue_tile(next_ref, 1 - slot)

    def drain(r, c):
        for k in range(TOP_K):
            row_copy(slot, 0, k, 0).wait()
        return c

    lax.fori_loop(0, tm, drain, 0)
    gates = gate_ref[...]
    y = gates[:, 0:1] * buf[slot, 0]
    for k in range(1, TOP_K):
        y = y + gates[:, k:k + 1] * buf[slot, k]
    o_ref[...] = _rms(x2_ref[...] + y, gf_ref[...])


def _combine_call(dest3, gates, x2, final_norm, expert_out):
    tm = COMBINE_ROWS
    T = x2.shape[0]
    n = T // tm
    row = pl.BlockSpec((tm, D_MODEL), lambda i: (i, 0))
    table = lambda f: pl.BlockSpec((1, 1, tm * TOP_K), f, memory_space=pltpu.SMEM)
    return pl.pallas_call(
        functools.partial(_combine_body, tm=tm),
        grid=(n,),
        in_specs=[table(lambda i: (i, 0, 0)), table(lambda i: (jnp.minimum(i + 1, n - 1), 0, 0)),
                  pl.BlockSpec((tm, LANES), lambda i: (i, 0)), row, _resident((1, D_MODEL)),
                  pl.BlockSpec(memory_space=pl.ANY)],
        out_specs=row,
        out_shape=jax.ShapeDtypeStruct((T, D_MODEL), F32),
        scratch_shapes=[pltpu.VMEM((2, TOP_K, tm, D_MODEL), F32), pltpu.SemaphoreType.DMA((2,))],
        compiler_params=_cparams(("arbitrary",)),
        name="moe_combine_norm",
    )(dest3, dest3, gates, x2, final_norm[None, :], expert_out)


def _routing_tables(idx, rank, counts, n_tokens):
    tk = n_tokens * TOP_K
    padded = (counts + MOE_ROW_TILE - 1) // MOE_ROW_TILE * MOE_ROW_TILE
    pad_end = jnp.cumsum(padded)
    pad_start = pad_end - padded
    onehot = idx[:, :, None] == jnp.arange(N_EXPERTS, dtype=I32)[None, None, :]
    dest = jnp.sum(jnp.where(onehot, pad_start[None, None, :], 0), axis=-1) + rank
    n_slots = (tk // MOE_ROW_TILE + N_EXPERTS) * MOE_ROW_TILE
    n_tiles = n_slots // MOE_ROW_TILE
    part_tile = jnp.where(counts % MOE_ROW_TILE != 0, pad_end // MOE_ROW_TILE - 1, -1)
    spare = pad_end[-1] // MOE_ROW_TILE + jnp.arange(N_EXPERTS, dtype=I32)
    fill_tile = jnp.concatenate([part_tile, jnp.where(spare < n_tiles, spare, -1)]).astype(I32)
    n_units = N_EXPERTS + tk // MOE_UNIT_ROWS
    units_per_e = (counts + MOE_UNIT_ROWS - 1) // MOE_UNIT_ROWS
    unit_end = jnp.cumsum(units_per_e)
    n_active = unit_end[-1]
    uid = jnp.arange(n_units, dtype=I32)
    ue = jnp.minimum(jnp.searchsorted(unit_end, uid, side='right'), N_EXPERTS - 1).astype(I32)
    k_in_e = uid - (unit_end - units_per_e)[ue]
    rows = jnp.clip(counts[ue] - k_in_e * MOE_UNIT_ROWS, 0, MOE_UNIT_ROWS)
    active = uid < n_active
    last_e = ue[jnp.maximum(n_active - 1, 0)]
    unit_e = jnp.where(active, ue, last_e).astype(I32)
    unit_rows = jnp.where(active, rows, 0).astype(I32)
    unit_s0 = jnp.where(active, pad_start[ue] + k_in_e * MOE_UNIT_ROWS, 0).astype(I32)
    used_tiles = pad_end[-1] // MOE_ROW_TILE
    tail = jnp.stack([used_tiles, n_slots // MOE_ROW_TILE - used_tiles]).astype(I32)
    return dest, fill_tile, unit_e, unit_rows, unit_s0, tail, n_slots, n_units


def kernel(x, mem, positions, attn_norm, w_in, q_a_norm, w_q_b, kv_a_norm, w_kv_b, ssm_lambda_re, ssm_lambda_im, ssm_log_dt, ssm_b_re, ssm_b_im, ssm_c_re, ssm_c_im, ssm_d, w_glu, b_glu, mix_norm_attn, mix_norm_ssm, w_out, xattn_norm, mem_norm, w_xq, w_xk, w_xv, w_xo, ffn_norm, w_router, b_router, w_up, b_up, w_down, b_down, final_norm):
    B, S, D = x.shape
    T = B * S
    assert w_in.shape[0] == 1, "single-layer block"
    l = 0
    xt = x.reshape(T, D)
    pos2d = positions.reshape(T, 1)
    mem2d = mem.reshape(B * MEM_LEN, D)
    q, k, v, u = _proj_call(xt, pos2d, attn_norm[l], w_in[l], q_a_norm[l], w_q_b[l],
                            kv_a_norm[l], w_kv_b[l])
    attn_o = _attn_call(q, k, v, B, S)
    ab_re, ab_im, bb_re, bb_im = _s5_param_call(ssm_lambda_re[l], ssm_lambda_im[l], ssm_log_dt[l],
                                                ssm_b_re[l], ssm_b_im[l])
    ssm_o = _s5_call(u, ab_re, ab_im, bb_re, bb_im, ssm_c_re[l], ssm_c_im[l], ssm_d[l],
                     w_glu[l], b_glu[l], B, S)
    x1, qx = _mix_call(xt, attn_o, ssm_o, mix_norm_attn[l], mix_norm_ssm[l], w_out[l],
                       xattn_norm[l], w_xq[l])
    kx, vx = _memkv_call(mem2d, mem_norm[l], w_xk[l], w_xv[l])
    x2, xn, xnp = _xattn_call(qx, kx, vx, x1, w_xo[l], ffn_norm[l], B, S)
    idx_w, gate_w, rank_w, counts_f = _router_call(xn, w_router[l], b_router[l])
    counts = counts_f[0].astype(I32)
    dest, fill_tile, unit_e, unit_rows, unit_s0, tail, n_slots, n_units = _routing_tables(
        idx_w[:, :TOP_K], rank_w[:, :TOP_K], counts, T)
    dest3 = dest.reshape(-1, 1, COMBINE_ROWS * TOP_K)
    xs = _dispatch_call(fill_tile, dest3, xnp, n_slots)
    expert_out = _expert_call(unit_e, unit_rows, unit_s0, tail, xs,
                              w_up[l], b_up[l], w_down[l], b_down[l], n_slots, n_units)
    out = _combine_call(dest3, gate_w, x2, final_norm, expert_out)
    return out.reshape(B, S, D)
```

```python
import functools
import math

import jax
import jax.numpy as jnp
from jax import lax
from jax.experimental import pallas as pl
from jax.experimental.pallas import tpu as pltpu

F32 = jnp.float32
BF16 = jnp.bfloat16
I32 = jnp.int32

D_MODEL = 2048
MEM_LEN = 256
MLA_HEADS = 8
QK_NOPE_DIM = 128
QK_ROPE_DIM = 64
V_HEAD_DIM = 128
Q_LORA_RANK = 384
KV_LORA_RANK = 256
ROPE_THETA = 10000.0
SSM_WIDTH = 1024
SSM_GROUP = 16
SSM_GROUPS = 64
SSM_STATE = 64
XATTN_HEADS = 4
XATTN_HEAD_DIM = D_MODEL // XATTN_HEADS
N_EXPERTS = 32
TOP_K = 4
D_FF = D_MODEL
SWIGLU_LIMIT = 7.0
SWIGLU_ALPHA = 1.702
EPS = 1e-6
COL_Q = Q_LORA_RANK
COL_KV = COL_Q + KV_LORA_RANK
COL_KPE = COL_KV + QK_ROPE_DIM

LANES = 128
SUBLANES = 8
MXU_DIM = 256
VMEM_LIMIT_BYTES = 56 * 1024 * 1024

HEAD_SLOT = 2 * LANES
ATTN_HEADS_PER_STEP = 2
STATE_ELEMS = SSM_GROUPS * SSM_STATE
STATE_SLABS = STATE_ELEMS // (SUBLANES * LANES)

MOE_ROW_TILE = 256
MOE_UNIT_ROWS = 1280
MOE_FF_CHUNK = 512
MOE_NJ = D_FF // MOE_FF_CHUNK
COMBINE_ROWS = 256
assert D_MODEL // 2 == SUBLANES * LANES
PACKED_TILE_ROWS = MOE_ROW_TILE * SUBLANES


def _cparams(semantics):
    return pltpu.CompilerParams(dimension_semantics=semantics,
                                vmem_limit_bytes=VMEM_LIMIT_BYTES)


def _resident(shape):
    nd = len(shape)
    return pl.BlockSpec(shape, lambda *_: (0,) * nd, pipeline_mode=pl.Buffered(1))


def _rms(xf, g):
    ms = jnp.mean(xf * xf, axis=-1, keepdims=True)
    return xf * lax.rsqrt(ms + EPS) * g


def _proj_body(x_ref, pos_ref, g_ref, w1_ref, qn_ref, wqb_ref, kvn_ref, wkvb_ref, freq_ref,
               q_ref, k_ref, v_ref, u_ref, *, scale):
    h = _rms(x_ref[...], g_ref[...]).astype(BF16)
    p = jnp.dot(h, w1_ref[...], preferred_element_type=F32)
    o_kv, o_ka, o_kb, o_u = COL_Q, COL_KV, COL_KV + LANES, COL_KV + 2 * LANES
    cq = p[:, :o_kv]
    ckv = p[:, o_kv:o_ka]
    kpa = p[:, o_ka:o_kb]
    kpb = p[:, o_kb:o_u]
    u_ref[...] = p[:, o_u:]
    ang = pos_ref[...].astype(F32) * freq_ref[...]
    c = jnp.cos(ang)
    s = jnp.sin(ang)
    qq = jnp.dot(_rms(cq, qn_ref[...]).astype(BF16), wqb_ref[...], preferred_element_type=F32)
    kv = jnp.dot(_rms(ckv, kvn_ref[...]).astype(BF16), wkvb_ref[...], preferred_element_type=F32)
    kpe = (kpa * c + kpb * s).astype(BF16)
    swap0 = MLA_HEADS * HEAD_SLOT
    for hd in range(MLA_HEADS):
        a0 = hd * HEAD_SLOT
        q_ref[:, a0:a0 + LANES] = (qq[:, a0:a0 + LANES] * scale).astype(BF16)
        qa = qq[:, a0 + LANES:a0 + HEAD_SLOT]
        qb = qq[:, swap0 + hd * LANES:swap0 + (hd + 1) * LANES]
        q_ref[:, a0 + LANES:a0 + HEAD_SLOT] = ((qa * c + qb * s) * scale).astype(BF16)
        k_ref[:, a0:a0 + LANES] = kv[:, hd * LANES:(hd + 1) * LANES].astype(BF16)
        k_ref[:, a0 + LANES:a0 + HEAD_SLOT] = kpe
    v_ref[...] = kv[:, MLA_HEADS * QK_NOPE_DIM:].astype(BF16)


def _rope_slot_pair(w):
    half = QK_ROPE_DIM // 2
    pad = jnp.zeros(w.shape[:-1] + (LANES - QK_ROPE_DIM,), w.dtype)
    a = jnp.concatenate([w, pad], axis=-1)
    b = jnp.concatenate([-w[..., half:], w[..., :half], pad], axis=-1)
    return a, b


def _proj_call(x2d, pos2d, attn_norm, w_in, q_a_norm, w_q_b, kv_a_norm, w_kv_b, tm=256):
    T = x2d.shape[0]
    kpa, kpb = _rope_slot_pair(w_in[:, COL_KV:COL_KPE])
    w1 = jnp.concatenate([w_in[:, :COL_KV], kpa, kpb, w_in[:, COL_KPE:]], axis=1).astype(BF16)
    wq = w_q_b.reshape(Q_LORA_RANK, MLA_HEADS, QK_NOPE_DIM + QK_ROPE_DIM)
    qpa, qpb = _rope_slot_pair(wq[..., QK_NOPE_DIM:])
    wqb = jnp.concatenate([
        jnp.concatenate([wq[..., :QK_NOPE_DIM], qpa], axis=-1).reshape(Q_LORA_RANK, MLA_HEADS * HEAD_SLOT),
        qpb.reshape(Q_LORA_RANK, MLA_HEADS * LANES)], axis=1).astype(BF16)
    wkv = w_kv_b.reshape(KV_LORA_RANK, MLA_HEADS, QK_NOPE_DIM + V_HEAD_DIM)
    wkvb = jnp.concatenate([wkv[..., :QK_NOPE_DIM].reshape(KV_LORA_RANK, -1),
                            wkv[..., QK_NOPE_DIM:].reshape(KV_LORA_RANK, -1)], axis=1).astype(BF16)
    inv_freq = 1.0 / (ROPE_THETA ** (jnp.arange(0, QK_ROPE_DIM, 2, dtype=F32) / QK_ROPE_DIM))
    freq = jnp.concatenate([inv_freq, inv_freq, jnp.zeros((LANES - QK_ROPE_DIM,), F32)])[None, :]
    scale = (QK_NOPE_DIM + QK_ROPE_DIM) ** -0.5
    n1 = w1.shape[1]
    row = lambda w: pl.BlockSpec((tm, w), lambda i: (i, 0))
    return pl.pallas_call(
        functools.partial(_proj_body, scale=scale),
        grid=(T // tm,),
        in_specs=[row(D_MODEL), row(1), _resident((1, D_MODEL)), _resident((D_MODEL, n1)),
                  _resident((1, Q_LORA_RANK)), _resident(wqb.shape),
                  _resident((1, KV_LORA_RANK)), _resident(wkvb.shape), _resident((1, LANES))],
        out_specs=[row(MLA_HEADS * HEAD_SLOT), row(MLA_HEADS * HEAD_SLOT),
                   row(MLA_HEADS * V_HEAD_DIM), row(SSM_WIDTH)],
        out_shape=[jax.ShapeDtypeStruct((T, MLA_HEADS * HEAD_SLOT), BF16),
                   jax.ShapeDtypeStruct((T, MLA_HEADS * HEAD_SLOT), BF16),
                   jax.ShapeDtypeStruct((T, MLA_HEADS * V_HEAD_DIM), BF16),
                   jax.ShapeDtypeStruct((T, SSM_WIDTH), F32)],
        compiler_params=_cparams(("parallel",)),
        name="proj_mla_rope",
    )(x2d, pos2d, attn_norm[None, :], w1, q_a_norm[None, :], wqb, kv_a_norm[None, :], wkvb, freq)


def _attn_body(q_ref, k_ref, v_ref, o_ref, *, tq):
    i = pl.program_id(2)

    def step(j, carry, diagonal):
        off = pl.multiple_of(j * tq, tq)
        out = []
        for hd in range(ATTN_HEADS_PER_STEP):
            m, l, acc = carry[hd]
            q = q_ref[:, hd * HEAD_SLOT:(hd + 1) * HEAD_SLOT]
            kj = k_ref[pl.ds(off, tq), hd * HEAD_SLOT:(hd + 1) * HEAD_SLOT]
            vj = v_ref[pl.ds(off, tq), hd * V_HEAD_DIM:(hd + 1) * V_HEAD_DIM]
            s = lax.dot_general(q, kj, (((1,), (1,)), ((), ())), preferred_element_type=F32)
            if diagonal:
                rows = lax.broadcasted_iota(I32, s.shape, 0)
                cols = lax.broadcasted_iota(I32, s.shape, 1)
                s = jnp.where(rows >= cols, s, -jnp.inf)
            m_new = jnp.maximum(m, jnp.max(s, axis=-1, keepdims=True))
            alpha = jnp.exp(m - m_new)
            p = jnp.exp(s - m_new)
            l = alpha * l + jnp.sum(p, axis=-1, keepdims=True)
            acc = alpha * acc + jnp.dot(p.astype(BF16), vj, preferred_element_type=F32)
            out.append((m_new, l, acc))
        return tuple(out)

    init = tuple((jnp.full((tq, 1), -jnp.inf, F32), jnp.zeros((tq, 1), F32),
                  jnp.zeros((tq, V_HEAD_DIM), F32)) for _ in range(ATTN_HEADS_PER_STEP))
    carry = lax.fori_loop(0, i, lambda j, c: step(j, c, False), init)
    carry = step(i, carry, True)
    for hd in range(ATTN_HEADS_PER_STEP):
        _, l, acc = carry[hd]
        o_ref[:, hd * V_HEAD_DIM:(hd + 1) * V_HEAD_DIM] = acc / l


def _attn_call(q, k, v, batch, seq, tq=512):
    nq = seq // tq
    hp = ATTN_HEADS_PER_STEP
    return pl.pallas_call(
        functools.partial(_attn_body, tq=tq),
        grid=(batch, MLA_HEADS // hp, nq),
        in_specs=[pl.BlockSpec((tq, hp * HEAD_SLOT), lambda b, h, i: (b * nq + i, h)),
                  pl.BlockSpec((seq, hp * HEAD_SLOT), lambda b, h, i: (b, h)),
                  pl.BlockSpec((seq, hp * V_HEAD_DIM), lambda b, h, i: (b, h))],
        out_specs=pl.BlockSpec((tq, hp * V_HEAD_DIM), lambda b, h, i: (b * nq + i, h)),
        out_shape=jax.ShapeDtypeStruct((batch * seq, MLA_HEADS * V_HEAD_DIM), F32),
        compiler_params=_cparams(("parallel", "parallel", "arbitrary")),
        name="mla_attention",
    )(q, k, v)


def _s5_param_body(lr_ref, li_ref, ldt_ref, br_ref, bi_ref, abr_ref, abi_ref, bbr_ref, bbi_ref):
    lr = lr_ref[...]
    li = li_ref[...]
    dt = jnp.exp(ldt_ref[...])
    mag = jnp.exp(lr * dt)
    ab_re = mag * jnp.cos(li * dt)
    ab_im = mag * jnp.sin(li * dt)
    nr, ni = ab_re - 1.0, ab_im
    den = lr * lr + li * li
    coef_re = (nr * lr + ni * li) / den
    coef_im = (ni * lr - nr * li) / den
    abr_ref[...] = ab_re
    abi_ref[...] = ab_im
    for c in range(SSM_GROUP):
        br = br_ref[c]
        bi = bi_ref[c]
        bbr_ref[c] = coef_re * br - coef_im * bi
        bbi_ref[c] = coef_re * bi + coef_im * br


def _s5_param_call(lam_re, lam_im, log_dt, b_re, b_im):
    G, P, C = SSM_GROUPS, SSM_STATE, SSM_GROUP
    gp = jax.ShapeDtypeStruct((G, P), F32)
    cgp = jax.ShapeDtypeStruct((C, G, P), F32)
    return pl.pallas_call(
        _s5_param_body, out_shape=[gp, gp, cgp, cgp], name="s5_discretise",
    )(lam_re, lam_im, log_dt[:, None], b_re.transpose(2, 0, 1), b_im.transpose(2, 0, 1))


def _s5_body(u_ref, bre_ref, bim_ref, cre_ref, cim_ref, ar_ref, ai_ref, d_ref, wg_ref, bg_ref,
             o_ref, sre, sim, st_re, st_im, *, tt):
    nk = SSM_WIDTH // MXU_DIM
    cols_per_k = STATE_ELEMS // nk
    lane_blocks = cols_per_k // LANES
    c_idx = pl.program_id(1)

    @pl.when(c_idx == 0)
    def _():
        st_re[...] = jnp.zeros_like(st_re)
        st_im[...] = jnp.zeros_like(st_im)

    u = u_ref[...]
    ub = u.astype(BF16)
    for k in range(nk):
        uk = ub[:, k * MXU_DIM:(k + 1) * MXU_DIM]
        pr = jnp.dot(uk, bre_ref[k], preferred_element_type=F32)
        pi = jnp.dot(uk, bim_ref[k], preferred_element_type=F32)
        for m in range(lane_blocks):
            blk = k * lane_blocks + m
            s_, j_ = blk // STATE_SLABS, blk % STATE_SLABS
            sre[j_, pl.ds(s_, tt, stride=SUBLANES), :] = pr[:, m * LANES:(m + 1) * LANES]
            sim[j_, pl.ds(s_, tt, stride=SUBLANES), :] = pi[:, m * LANES:(m + 1) * LANES]

    ar = [ar_ref[j] for j in range(STATE_SLABS)]
    ai = [ai_ref[j] for j in range(STATE_SLABS)]

    def scan_step(t, carry):
        xr, xi = carry
        row = pl.multiple_of(t * SUBLANES, SUBLANES)
        nxr, nxi = [], []
        for j in range(STATE_SLABS):
            br = sre[j, pl.ds(row, SUBLANES), :]
            bi = sim[j, pl.ds(row, SUBLANES), :]
            r = ar[j] * xr[j] - ai[j] * xi[j] + br
            im = ar[j] * xi[j] + ai[j] * xr[j] + bi
            sre[j, pl.ds(row, SUBLANES), :] = r
            sim[j, pl.ds(row, SUBLANES), :] = im
            nxr.append(r)
            nxi.append(im)
        return tuple(nxr), tuple(nxi)

    x0 = (tuple(st_re[j] for j in range(STATE_SLABS)), tuple(st_im[j] for j in range(STATE_SLABS)))
    xr, xi = lax.fori_loop(0, tt, scan_step, x0, unroll=8)
    for j in range(STATE_SLABS):
        st_re[j] = xr[j]
        st_im[j] = xi[j]

    ys = []
    for k in range(nk):
        xre, xim = [], []
        for m in range(lane_blocks):
            blk = k * lane_blocks + m
            s_, j_ = blk // STATE_SLABS, blk % STATE_SLABS
            xre.append(sre[j_, pl.ds(s_, tt, stride=SUBLANES), :])
            xim.append(sim[j_, pl.ds(s_, tt, stride=SUBLANES), :])
        xre = jnp.concatenate(xre, axis=1).astype(BF16)
        xim = jnp.concatenate(xim, axis=1).astype(BF16)
        ys.append(jnp.dot(xre, cre_ref[k], preferred_element_type=F32)
                  - jnp.dot(xim, cim_ref[k], preferred_element_type=F32))
    y = jnp.concatenate(ys, axis=1) + d_ref[...] * u
    y = jax.nn.gelu(y)
    gate = jax.nn.sigmoid(jnp.dot(y.astype(BF16), wg_ref[...], preferred_element_type=F32) + bg_ref[...])
    o_ref[...] = y * gate


def _block_diag(blocks):
    nk, ng, r, c = blocks.shape
    eye = jnp.eye(ng, dtype=blocks.dtype)
    out = blocks[:, :, :, None, :] * eye[None, :, None, :, None]
    return out.reshape(nk, ng * r, ng * c)


def _s5_call(u, ab_re, ab_im, bb_re, bb_im, c_re, c_im, d_skip, w_glu, b_glu, batch, seq, tt=256):
    G, P, C = SSM_GROUPS, SSM_STATE, SSM_GROUP
    nk = SSM_WIDTH // MXU_DIM
    gk = G // nk
    def bmat(bb):
        blocks = bb.transpose(1, 0, 2).reshape(nk, gk, C, P)
        return _block_diag(blocks).astype(BF16)
    def cmat(cc):
        blocks = cc.transpose(0, 2, 1).reshape(nk, gk, P, C)
        return _block_diag(blocks).astype(BF16)
    def slab(a):
        return a.reshape(SUBLANES, STATE_SLABS, LANES).transpose(1, 0, 2)
    nc = seq // tt
    row = pl.BlockSpec((tt, SSM_WIDTH), lambda b, c: (b * nc + c, 0))
    return pl.pallas_call(
        functools.partial(_s5_body, tt=tt),
        grid=(batch, nc),
        in_specs=[row, _resident((nk, MXU_DIM, STATE_ELEMS // nk)), _resident((nk, MXU_DIM, STATE_ELEMS // nk)),
                  _resident((nk, STATE_ELEMS // nk, MXU_DIM)), _resident((nk, STATE_ELEMS // nk, MXU_DIM)),
                  _resident((STATE_SLABS, SUBLANES, LANES)), _resident((STATE_SLABS, SUBLANES, LANES)),
                  _resident((1, SSM_WIDTH)), _resident((SSM_WIDTH, SSM_WIDTH)), _resident((1, SSM_WIDTH))],
        out_specs=row,
        out_shape=jax.ShapeDtypeStruct((batch * seq, SSM_WIDTH), F32),
        scratch_shapes=[pltpu.VMEM((STATE_SLABS, tt * SUBLANES, LANES), F32),
                        pltpu.VMEM((STATE_SLABS, tt * SUBLANES, LANES), F32),
                        pltpu.VMEM((STATE_SLABS, SUBLANES, LANES), F32),
                        pltpu.VMEM((STATE_SLABS, SUBLANES, LANES), F32)],
        compiler_params=_cparams(("parallel", "arbitrary")),
        name="s5_scan_glu",
    )(u, bmat(bb_re), bmat(bb_im), cmat(c_re), cmat(c_im), slab(ab_re), slab(ab_im),
      d_skip[None, :], w_glu.astype(BF16), b_glu[None, :])


def _mix_body(x_ref, ao_ref, so_ref, ga_ref, gs_ref, woa_ref, wos_ref, gx_ref, wxq_ref,
              x1_ref, qx_ref, *, xscale):
    a = _rms(ao_ref[...], ga_ref[...]).astype(BF16)
    s = _rms(so_ref[...], gs_ref[...]).astype(BF16)
    x1 = (x_ref[...] + jnp.dot(a, woa_ref[...], preferred_element_type=F32)
          + jnp.dot(s, wos_ref[...], preferred_element_type=F32))
    x1_ref[...] = x1
    hq = _rms(x1, gx_ref[...]).astype(BF16)
    qx_ref[...] = (jnp.dot(hq, wxq_ref[...], preferred_element_type=F32) * xscale).astype(BF16)


def _mix_call(x2d, attn_o, ssm_o, g_attn, g_ssm, w_out, g_x, w_xq, tm=256):
    T = x2d.shape[0]
    na = MLA_HEADS * V_HEAD_DIM
    row = lambda w: pl.BlockSpec((tm, w), lambda i: (i, 0))
    return pl.pallas_call(
        functools.partial(_mix_body, xscale=XATTN_HEAD_DIM ** -0.5),
        grid=(T // tm,),
        in_specs=[row(D_MODEL), row(na), row(SSM_WIDTH), _resident((1, na)), _resident((1, SSM_WIDTH)),
                  _resident((na, D_MODEL)), _resident((SSM_WIDTH, D_MODEL)),
                  _resident((1, D_MODEL)), _resident((D_MODEL, D_MODEL))],
        out_specs=[row(D_MODEL), row(D_MODEL)],
        out_shape=[jax.ShapeDtypeStruct((T, D_MODEL), F32), jax.ShapeDtypeStruct((T, D_MODEL), BF16)],
        compiler_params=_cparams(("parallel",)),
        name="mix_out_proj",
    )(x2d, attn_o, ssm_o, g_attn[None, :], g_ssm[None, :], w_out[:na].astype(BF16),
      w_out[na:].astype(BF16), g_x[None, :], w_xq.astype(BF16))


def _memkv_body(m_ref, g_ref, wk_ref, wv_ref, k_ref, v_ref):
    h = _rms(m_ref[...], g_ref[...]).astype(BF16)
    k_ref[...] = jnp.dot(h, wk_ref[...], preferred_element_type=F32).astype(BF16)
    v_ref[...] = jnp.dot(h, wv_ref[...], preferred_element_type=F32).astype(BF16)


def _memkv_call(mem2d, g_mem, w_xk, w_xv, tm=256):
    R = mem2d.shape[0]
    row = pl.BlockSpec((tm, D_MODEL), lambda i: (i, 0))
    return pl.pallas_call(
        _memkv_body,
        grid=(R // tm,),
        in_specs=[row, _resident((1, D_MODEL)), _resident((D_MODEL, D_MODEL)), _resident((D_MODEL, D_MODEL))],
        out_specs=[row, row],
        out_shape=[jax.ShapeDtypeStruct((R, D_MODEL), BF16)] * 2,
        compiler_params=_cparams(("parallel",)),
        name="mem_kv_proj",
    )(mem2d, g_mem[None, :], w_xk.astype(BF16), w_xv.astype(BF16))


def _pack_bf16_pairs(xf):
    c = xf.shape[1] // 2
    bits = pltpu.bitcast(xf.astype(BF16).astype(F32), jnp.uint32)
    return lax.shift_right_logical(bits[:, :c], jnp.uint32(16)) | bits[:, c:]


def _unpack_bf16_pairs(w):
    lo = pltpu.bitcast(lax.shift_left(w, jnp.uint32(16)), F32).astype(BF16)
    hi = pltpu.bitcast(w & jnp.uint32(0xFFFF0000), F32).astype(BF16)
    return lo, hi


def _xattn_body(q_ref, k_ref, v_ref, x1_ref, wo_ref, gf_ref, x2_ref, xn_ref, xp_ref):
    outs = []
    for h in range(XATTN_HEADS):
        sl = slice(h * XATTN_HEAD_DIM, (h + 1) * XATTN_HEAD_DIM)
        s = lax.dot_general(q_ref[:, sl], k_ref[:, sl], (((1,), (1,)), ((), ())),
                            preferred_element_type=F32)
        p = jnp.exp(s - jnp.max(s, axis=-1, keepdims=True))
        denom = jnp.sum(p, axis=-1, keepdims=True)
        o = jnp.dot(p.astype(BF16), v_ref[:, sl], preferred_element_type=F32) / denom
        outs.append(o.astype(BF16))
    o = jnp.concatenate(outs, axis=1)
    x2 = x1_ref[...] + jnp.dot(o, wo_ref[...], preferred_element_type=F32)
    x2_ref[...] = x2
    xn = _rms(x2, gf_ref[...])
    xn_ref[...] = xn
    packed = _pack_bf16_pairs(xn)
    tq = packed.shape[0]
    for s in range(SUBLANES):
        xp_ref[pl.ds(s, tq, stride=SUBLANES), :] = packed[:, s * LANES:(s + 1) * LANES]


def _xattn_call(qx, kx, vx, x1, w_xo, g_ffn, batch, seq, tq=256):
    nq = seq // tq
    row = pl.BlockSpec((tq, D_MODEL), lambda b, i: (b * nq + i, 0))
    half = pl.BlockSpec((tq * SUBLANES, LANES), lambda b, i: (b * nq + i, 0))
    memb = pl.BlockSpec((MEM_LEN, D_MODEL), lambda b, i: (b, 0))
    return pl.pallas_call(
        _xattn_body,
        grid=(batch, nq),
        in_specs=[row, memb, memb, row, _resident((D_MODEL, D_MODEL)), _resident((1, D_MODEL))],
        out_specs=[row, row, half],
        out_shape=[jax.ShapeDtypeStruct((batch * seq, D_MODEL), F32),
                   jax.ShapeDtypeStruct((batch * seq, D_MODEL), F32),
                   jax.ShapeDtypeStruct((batch * seq * SUBLANES, LANES), jnp.uint32)],
        compiler_params=_cparams(("parallel", "parallel")),
        name="cross_attention",
    )(qx, kx, vx, x1, w_xo.astype(BF16), g_ffn[None, :])


def _router_body(xn_ref, wr_ref, br_ref, idx_ref, gate_ref, rank_ref, cnt_ref, carry, *, tm):
    @pl.when(pl.program_id(0) == 0)
    def _():
        carry[...] = jnp.zeros_like(carry)

    logits = jnp.dot(xn_ref[...], wr_ref[...], preferred_element_type=F32,
                     precision=lax.Precision.HIGHEST) + br_ref[...]
    lane = lax.broadcasted_iota(I32, logits.shape, 1)
    work = logits
    sels, vals, ids = [], [], []
    for _ in range(TOP_K):
        mx = jnp.max(work, axis=-1, keepdims=True)
        am = jnp.min(jnp.where(work == mx, lane, N_EXPERTS), axis=-1, keepdims=True)
        sel = lane == am
        sels.append(sel)
        vals.append(mx)
        ids.append(am)
        work = jnp.where(sel, -jnp.inf, work)
    es = [jnp.exp(v - vals[0]) for v in vals]
    denom = es[0] + es[1] + es[2] + es[3]
    member = jnp.zeros(logits.shape, F32)
    for sel in sels:
        member = member + jnp.where(sel, 1.0, 0.0)
    r_i = lax.broadcasted_iota(I32, (tm, tm), 0)
    c_i = lax.broadcasted_iota(I32, (tm, tm), 1)
    tri = jnp.where(r_i > c_i, 1.0, 0.0).astype(BF16)
    before = jnp.dot(tri, member.astype(BF16), preferred_element_type=F32) + carry[...]
    out_lane = lax.broadcasted_iota(I32, (tm, LANES), 1)
    idx_o = jnp.zeros((tm, LANES), I32)
    gate_o = jnp.zeros((tm, LANES), F32)
    rank_o = jnp.zeros((tm, LANES), I32)
    for k in range(TOP_K):
        rk = jnp.sum(jnp.where(sels[k], before, 0.0), axis=-1, keepdims=True).astype(I32)
        idx_o = jnp.where(out_lane == k, ids[k], idx_o)
        gate_o = jnp.where(out_lane == k, es[k] / denom, gate_o)
        rank_o = jnp.where(out_lane == k, rk, rank_o)
    idx_ref[...] = idx_o
    gate_ref[...] = gate_o
    rank_ref[...] = rank_o
    carry[...] = carry[...] + jnp.sum(member, axis=0, keepdims=True)
    cnt_ref[...] = carry[...]


def _router_call(xn, w_router, b_router, tm=256):
    T = xn.shape[0]
    wide = pl.BlockSpec((tm, LANES), lambda i: (i, 0))
    return pl.pallas_call(
        functools.partial(_router_body, tm=tm),
        grid=(T // tm,),
        in_specs=[pl.BlockSpec((tm, D_MODEL), lambda i: (i, 0)),
                  _resident((D_MODEL, N_EXPERTS)), _resident((1, N_EXPERTS))],
        out_specs=[wide, wide, wide, pl.BlockSpec((1, N_EXPERTS), lambda i: (0, 0))],
        out_shape=[jax.ShapeDtypeStruct((T, LANES), I32), jax.ShapeDtypeStruct((T, LANES), F32),
                   jax.ShapeDtypeStruct((T, LANES), I32), jax.ShapeDtypeStruct((1, N_EXPERTS), F32)],
        scratch_shapes=[pltpu.VMEM((1, N_EXPERTS), F32)],
        compiler_params=_cparams(("arbitrary",)),
        name="moe_router",
    )(xn, w_router, b_router[None, :])


def _dispatch_body(fill_ref, dest_ref, xp_ref, xs_hbm, zbuf, sems, *, tm):
    @pl.when(pl.program_id(0) == 0)
    def _():
        zbuf[...] = jnp.zeros_like(zbuf)

        def zero_copy(e):
            dst = pl.multiple_of(fill_ref[e] * PACKED_TILE_ROWS, PACKED_TILE_ROWS)
            return pltpu.make_async_copy(zbuf, xs_hbm.at[pl.ds(dst, PACKED_TILE_ROWS)], sems.at[0])

        def start(e, c):
            @pl.when(fill_ref[e] >= 0)
            def _():
                zero_copy(e).start()
            return c

        def wait(e, c):
            @pl.when(fill_ref[e] >= 0)
            def _():
                zero_copy(e).wait()
            return c

        lax.fori_loop(0, 2 * N_EXPERTS, start, 0)
        lax.fori_loop(0, 2 * N_EXPERTS, wait, 0)

    def row_copy(r, dst):
        src = pl.multiple_of(r * SUBLANES, SUBLANES)
        dst = pl.multiple_of(dst * SUBLANES, SUBLANES)
        return pltpu.make_async_copy(xp_ref.at[pl.ds(src, SUBLANES)], xs_hbm.at[pl.ds(dst, SUBLANES)],
                                     sems.at[1])

    def issue(r, c):
        for k in range(TOP_K):
            row_copy(r, dest_ref[0, 0, r * TOP_K + k]).start(priority=k % 2)
        return c

    lax.fori_loop(0, tm, issue, 0)

    def drain(r, c):
        for k in range(TOP_K):
            row_copy(0, 0).wait()
        return c

    lax.fori_loop(0, tm, drain, 0)


def _dispatch_call(fill_tile, dest3, xnp, n_slots):
    tm = COMBINE_ROWS
    T = xnp.shape[0] // SUBLANES
    grid_spec = pltpu.PrefetchScalarGridSpec(
        num_scalar_prefetch=1,
        grid=(T // tm,),
        in_specs=[pl.BlockSpec((1, 1, tm * TOP_K), lambda i, fl: (i, 0, 0), memory_space=pltpu.SMEM),
                  pl.BlockSpec((tm * SUBLANES, LANES), lambda i, fl: (i, 0))],
        out_specs=pl.BlockSpec(memory_space=pl.ANY),
        scratch_shapes=[pltpu.VMEM((PACKED_TILE_ROWS, LANES), jnp.uint32), pltpu.SemaphoreType.DMA((2,))],
    )
    return pl.pallas_call(
        functools.partial(_dispatch_body, tm=tm),
        grid_spec=grid_spec,
        out_shape=jax.ShapeDtypeStruct((n_slots * SUBLANES, LANES), jnp.uint32),
        compiler_params=_cparams(("arbitrary",)),
        name="moe_dispatch",
    )(fill_tile, dest3, xnp)


def _expert_body(ue_ref, ur_ref, us_ref, tail_ref, xs_hbm, wg_ref, wl_ref, bg_ref, bl_ref,
                 wd_ref, bd_ref, out_hbm, raw, acc, wg_bf, wl_bf, wd_bf, sems):
    u = pl.program_id(0)
    j = pl.program_id(1)
    rows = ur_ref[u]
    s0 = us_ref[u]
    nsub = (rows + (MOE_ROW_TILE - 1)) // MOE_ROW_TILE
    half = D_MODEL // 2

    @pl.when(jnp.logical_and(j == 0, rows > 0))
    def _():
        def in_copy(r):
            off = pl.multiple_of(r * PACKED_TILE_ROWS, PACKED_TILE_ROWS)
            src = pl.multiple_of((s0 + r * MOE_ROW_TILE) * SUBLANES, PACKED_TILE_ROWS)
            return pltpu.make_async_copy(xs_hbm.at[pl.ds(src, PACKED_TILE_ROWS)],
                                         raw.at[pl.ds(off, PACKED_TILE_ROWS)], sems.at[0])

        def start(r, c):
            in_copy(r).start()
            return c

        lax.fori_loop(0, nsub, start, 0)
        bd = jnp.broadcast_to(bd_ref[0], (MOE_ROW_TILE, D_MODEL))

        def init(r, c):
            off = pl.multiple_of(r * MOE_ROW_TILE, MOE_ROW_TILE)
            acc[pl.ds(off, MOE_ROW_TILE), :] = bd
            return c

        def wait(r, c):
            in_copy(r).wait()
            return c

        lax.fori_loop(0, nsub, init, 0)
        lax.fori_loop(0, nsub, wait, 0)

    bg = bg_ref[0]
    bl = bl_ref[0]

    last_step = j == MOE_NJ - 1
    pair = 2 * MOE_ROW_TILE
    npair = nsub // 2
    odd = nsub - 2 * npair

    def up(off, n, wg, wl):
        base = off * SUBLANES
        parts = [_unpack_bf16_pairs(raw[pl.ds(base + s, n, stride=SUBLANES), :]) for s in range(SUBLANES)]
        lo = jnp.concatenate([p[0] for p in parts], axis=1)
        hi = jnp.concatenate([p[1] for p in parts], axis=1)
        g = (jnp.dot(lo, wg[:half], preferred_element_type=F32)
             + jnp.dot(hi, wg[half:], preferred_element_type=F32) + bg)
        l = (jnp.dot(lo, wl[:half], preferred_element_type=F32)
             + jnp.dot(hi, wl[half:], preferred_element_type=F32) + bl)
        glu = jnp.minimum(g, SWIGLU_LIMIT)
        lin = jnp.clip(l, -SWIGLU_LIMIT, SWIGLU_LIMIT)
        return (glu * jax.nn.sigmoid(SWIGLU_ALPHA * glu) * (lin + 1.0)).astype(BF16)

    def down(off, n, act, wd):
        acc[pl.ds(off, n), :] += jnp.dot(act, wd, preferred_element_type=F32)

    def out_copy(off, n):
        dst = pl.multiple_of(s0 + off, MOE_ROW_TILE)
        return pltpu.make_async_copy(acc.at[pl.ds(off, n)], out_hbm.at[pl.ds(dst, n)], sems.at[2])

    def send(off, n):
        @pl.when(last_step)
        def _():
            out_copy(off, n).start()

    def cast_weights():
        wg = wg_ref[0].astype(BF16)
        wl = wl_ref[0].astype(BF16)
        wd = wd_ref[0].astype(BF16)
        wg_bf[...] = wg
        wl_bf[...] = wl
        wd_bf[...] = wd
        return wg, wl

    @pl.when(npair > 0)
    def _():
        wg, wl = cast_weights()
        act0 = up(0, pair, wg, wl)

        def body(r, act):
            off = pl.multiple_of(r * pair, pair)
            nxt = pl.multiple_of(off + pair, pair)
            down(off, pair, act, wd_bf[...])
            act = up(nxt, pair, wg_bf[...], wl_bf[...])
            send(off, pair)
            return act

        act_last = lax.fori_loop(0, npair - 1, body, act0)
        off = pl.multiple_of((npair - 1) * pair, pair)
        down(off, pair, act_last, wd_bf[...])
        send(off, pair)

    @pl.when(odd > 0)
    def _():
        @pl.when(npair == 0)
        def _():
            cast_weights()

        off = pl.multiple_of(npair * pair, pair)
        down(off, MOE_ROW_TILE, up(off, MOE_ROW_TILE, wg_bf[...], wl_bf[...]), wd_bf[...])
        send(off, MOE_ROW_TILE)

    @pl.when(last_step)
    def _():
        def wait_pair(r, c):
            out_copy(0, pair).wait()
            return c

        lax.fori_loop(0, npair, wait_pair, 0)

        @pl.when(odd > 0)
        def _():
            out_copy(0, MOE_ROW_TILE).wait()


    @pl.when(jnp.logical_and(u == pl.num_programs(0) - 1, j == MOE_NJ - 1))
    def _():
        acc[pl.ds(0, MOE_ROW_TILE), :] = jnp.zeros((MOE_ROW_TILE, D_MODEL), F32)
        first = tail_ref[0]
        count = tail_ref[1]

        def zero_copy(t):
            dst = pl.multiple_of((first + t) * MOE_ROW_TILE, MOE_ROW_TILE)
            return pltpu.make_async_copy(acc.at[pl.ds(0, MOE_ROW_TILE)],
                                         out_hbm.at[pl.ds(dst, MOE_ROW_TILE)], sems.at[2])

        def start(t, c):
            zero_copy(t).start()
            return c

        def wait(t, c):
            zero_copy(t).wait()
            return c

        lax.fori_loop(0, count, start, 0)
        lax.fori_loop(0, count, wait, 0)


def _expert_call(unit_e, unit_rows, unit_s0, tail, xs, w_up, b_up, w_down, b_down, n_slots, n_units):
    fc = MOE_FF_CHUNK
    grid_spec = pltpu.PrefetchScalarGridSpec(
        num_scalar_prefetch=4,
        grid=(n_units, MOE_NJ),
        in_specs=[
            pl.BlockSpec(memory_space=pl.ANY),
            pl.BlockSpec((1, D_MODEL, fc), lambda u, j, ue, ur, us, tl: (ue[u], 0, j)),
            pl.BlockSpec((1, D_MODEL, fc), lambda u, j, ue, ur, us, tl: (ue[u], 0, MOE_NJ + j)),
            pl.BlockSpec((1, 1, fc), lambda u, j, ue, ur, us, tl: (ue[u], 0, j)),
            pl.BlockSpec((1, 1, fc), lambda u, j, ue, ur, us, tl: (ue[u], 0, MOE_NJ + j)),
            pl.BlockSpec((1, fc, D_MODEL), lambda u, j, ue, ur, us, tl: (ue[u], j, 0)),
            pl.BlockSpec((1, 1, D_MODEL), lambda u, j, ue, ur, us, tl: (ue[u], 0, 0)),
        ],
        out_specs=pl.BlockSpec(memory_space=pl.ANY),
        scratch_shapes=[
            pltpu.VMEM((MOE_UNIT_ROWS * SUBLANES, LANES), jnp.uint32),
            pltpu.VMEM((MOE_UNIT_ROWS, D_MODEL), F32),
            pltpu.VMEM((D_MODEL, fc), BF16),
            pltpu.VMEM((D_MODEL, fc), BF16),
            pltpu.VMEM((fc, D_MODEL), BF16),
            pltpu.SemaphoreType.DMA((3,)),
        ],
    )
    return pl.pallas_call(
        _expert_body,
        grid_spec=grid_spec,
        out_shape=jax.ShapeDtypeStruct((n_slots, D_MODEL), F32),
        compiler_params=_cparams(("arbitrary", "arbitrary")),
        name="moe_experts",
    )(unit_e, unit_rows, unit_s0, tail, xs, w_up, w_up, b_up[:, None, :], b_up[:, None, :],
      w_down, b_down[:, None, :])


def _combine_body(dest_ref, next_ref, gate_ref, x2_ref, gf_ref, out_hbm, o_ref, buf, sems, *, tm):
    i = pl.program_id(0)
    n = pl.num_programs(0)
    slot = lax.rem(i, 2)

    def row_copy(sl, r, k, src):
        return pltpu.make_async_copy(out_hbm.at[pl.ds(src, 1)], buf.at[sl, k, pl.ds(r, 1)], sems.at[sl])

    def issue_tile(table, sl):
        def issue(r, c):
            for k in range(TOP_K):
                row_copy(sl, r, k, table[0, 0, r * TOP_K + k]).start(priority=k % 2)
            return c

        lax.fori_loop(0, tm, issue, 0)

    @pl.when(i == 0)
    def _():
        issue_tile(dest_ref, 0)

    @pl.when(i + 1 < n)
    def _():
        issue_tile(next_ref, 1 - slot)

    def drain(r, c):
        for k in range(TOP_K):
            row_copy(slot, 0, k, 0).wait()
        return c

    lax.fori_loop(0, tm, drain, 0)
    gates = gate_ref[...]
    y = gates[:, 0:1] * buf[slot, 0]
    for k in range(1, TOP_K):
        y = y + gates[:, k:k + 1] * buf[slot, k]
    o_ref[...] = _rms(x2_ref[...] + y, gf_ref[...])


def _combine_call(dest3, gates, x2, final_norm, expert_out):
    tm = COMBINE_ROWS
    T = x2.shape[0]
    n = T // tm
    row = pl.BlockSpec((tm, D_MODEL), lambda i: (i, 0))
    table = lambda f: pl.BlockSpec((1, 1, tm * TOP_K), f, memory_space=pltpu.SMEM)
    return pl.pallas_call(
        functools.partial(_combine_body, tm=tm),
        grid=(n,),
        in_specs=[table(lambda i: (i, 0, 0)), table(lambda i: (jnp.minimum(i + 1, n - 1), 0, 0)),
                  pl.BlockSpec((tm, LANES), lambda i: (i, 0)), row, _resident((1, D_MODEL)),
                  pl.BlockSpec(memory_space=pl.ANY)],
        out_specs=row,
        out_shape=jax.ShapeDtypeStruct((T, D_MODEL), F32),
        scratch_shapes=[pltpu.VMEM((2, TOP_K, tm, D_MODEL), F32), pltpu.SemaphoreType.DMA((2,))],
        compiler_params=_cparams(("arbitrary",)),
        name="moe_combine_norm",
    )(dest3, dest3, gates, x2, final_norm[None, :], expert_out)


def _routing_tables(idx, rank, counts, n_tokens):
    tk = n_tokens * TOP_K
    padded = (counts + MOE_ROW_TILE - 1) // MOE_ROW_TILE * MOE_ROW_TILE
    pad_end = jnp.cumsum(padded)
    pad_start = pad_end - padded
    onehot = idx[:, :, None] == jnp.arange(N_EXPERTS, dtype=I32)[None, None, :]
    dest = jnp.sum(jnp.where(onehot, pad_start[None, None, :], 0), axis=-1) + rank
    n_slots = (tk // MOE_ROW_TILE + N_EXPERTS) * MOE_ROW_TILE
    n_tiles = n_slots // MOE_ROW_TILE
    part_tile = jnp.where(counts % MOE_ROW_TILE != 0, pad_end // MOE_ROW_TILE - 1, -1)
    spare = pad_end[-1] // MOE_ROW_TILE + jnp.arange(N_EXPERTS, dtype=I32)
    fill_tile = jnp.concatenate([part_tile, jnp.where(spare < n_tiles, spare, -1)]).astype(I32)
    n_units = N_EXPERTS + tk // MOE_UNIT_ROWS
    units_per_e = (counts + MOE_UNIT_ROWS - 1) // MOE_UNIT_ROWS
    unit_end = jnp.cumsum(units_per_e)
    n_active = unit_end[-1]
    uid = jnp.arange(n_units, dtype=I32)
    ue = jnp.minimum(jnp.searchsorted(unit_end, uid, side='right'), N_EXPERTS - 1).astype(I32)
    k_in_e = uid - (unit_end - units_per_e)[ue]
    rows = jnp.clip(counts[ue] - k_in_e * MOE_UNIT_ROWS, 0, MOE_UNIT_ROWS)
    active = uid < n_active
    last_e = ue[jnp.maximum(n_active - 1, 0)]
    unit_e = jnp.where(active, ue, last_e).astype(I32)
    unit_rows = jnp.where(active, rows, 0).astype(I32)
    unit_s0 = jnp.where(active, pad_start[ue] + k_in_e * MOE_UNIT_ROWS, 0).astype(I32)
    used_tiles = pad_end[-1] // MOE_ROW_TILE
    tail = jnp.stack([used_tiles, n_slots // MOE_ROW_TILE - used_tiles]).astype(I32)
    return dest, fill_tile, unit_e, unit_rows, unit_s0, tail, n_slots, n_units


def kernel(x, mem, positions, attn_norm, w_in, q_a_norm, w_q_b, kv_a_norm, w_kv_b, ssm_lambda_re, ssm_lambda_im, ssm_log_dt, ssm_b_re, ssm_b_im, ssm_c_re, ssm_c_im, ssm_d, w_glu, b_glu, mix_norm_attn, mix_norm_ssm, w_out, xattn_norm, mem_norm, w_xq, w_xk, w_xv, w_xo, ffn_norm, w_router, b_router, w_up, b_up, w_down, b_down, final_norm):
    B, S, D = x.shape
    T = B * S
    assert w_in.shape[0] == 1, "single-layer block"
    l = 0
    xt = x.reshape(T, D)
    pos2d = positions.reshape(T, 1)
    mem2d = mem.reshape(B * MEM_LEN, D)
    q, k, v, u = _proj_call(xt, pos2d, attn_norm[l], w_in[l], q_a_norm[l], w_q_b[l],
                            kv_a_norm[l], w_kv_b[l])
    attn_o = _attn_call(q, k, v, B, S)
    ab_re, ab_im, bb_re, bb_im = _s5_param_call(ssm_lambda_re[l], ssm_lambda_im[l], ssm_log_dt[l],
                                                ssm_b_re[l], ssm_b_im[l])
    ssm_o = _s5_call(u, ab_re, ab_im, bb_re, bb_im, ssm_c_re[l], ssm_c_im[l], ssm_d[l],
                     w_glu[l], b_glu[l], B, S)
    x1, qx = _mix_call(xt, attn_o, ssm_o, mix_norm_attn[l], mix_norm_ssm[l], w_out[l],
                       xattn_norm[l], w_xq[l])
    kx, vx = _memkv_call(mem2d, mem_norm[l], w_xk[l], w_xv[l])
    x2, xn, xnp = _xattn_call(qx, kx, vx, x1, w_xo[l], ffn_norm[l], B, S)
    idx_w, gate_w, rank_w, counts_f = _router_call(xn, w_router[l], b_router[l])
    counts = counts_f[0].astype(I32)
    dest, fill_tile, unit_e, unit_rows, unit_s0, tail, n_slots, n_units = _routing_tables(
        idx_w[:, :TOP_K], rank_w[:, :TOP_K], counts, T)
    dest3 = dest.reshape(-1, 1, COMBINE_ROWS * TOP_K)
    xs = _dispatch_call(fill_tile, dest3, xnp, n_slots)
    expert_out = _expert_call(unit_e, unit_rows, unit_s0, tail, xs,
                              w_up[l], b_up[l], w_down[l], b_down[l], n_slots, n_units)
    out = _combine_call(dest3, gate_w, x2, final_norm, expert_out)
    return out.reshape(B, S, D)
```

```python
import functools
import math

import jax
import jax.numpy as jnp
from jax import lax
from jax.experimental import pallas as pl
from jax.experimental.pallas import tpu as pltpu

F32 = jnp.float32
BF16 = jnp.bfloat16
I32 = jnp.int32

D_MODEL = 2048
MEM_LEN = 256
MLA_HEADS = 8
QK_NOPE_DIM = 128
QK_ROPE_DIM = 64
V_HEAD_DIM = 128
Q_LORA_RANK = 384
KV_LORA_RANK = 256
ROPE_THETA = 10000.0
SSM_WIDTH = 1024
SSM_GROUP = 16
SSM_GROUPS = 64
SSM_STATE = 64
XATTN_HEADS = 4
XATTN_HEAD_DIM = D_MODEL // XATTN_HEADS
N_EXPERTS = 32
TOP_K = 4
D_FF = D_MODEL
SWIGLU_LIMIT = 7.0
SWIGLU_ALPHA = 1.702
EPS = 1e-6
COL_Q = Q_LORA_RANK
COL_KV = COL_Q + KV_LORA_RANK
COL_KPE = COL_KV + QK_ROPE_DIM

LANES = 128
SUBLANES = 8
MXU_DIM = 256
VMEM_LIMIT_BYTES = 56 * 1024 * 1024

HEAD_SLOT = 2 * LANES
ATTN_HEADS_PER_STEP = 2
STATE_ELEMS = SSM_GROUPS * SSM_STATE
STATE_SLABS = STATE_ELEMS // (SUBLANES * LANES)

MOE_ROW_TILE = 256
MOE_UNIT_ROWS = 1280
MOE_FF_CHUNK = 512
MOE_NJ = D_FF // MOE_FF_CHUNK
COMBINE_ROWS = 256
assert D_MODEL // 2 == SUBLANES * LANES
PACKED_TILE_ROWS = MOE_ROW_TILE * SUBLANES


def _cparams(semantics):
    return pltpu.CompilerParams(dimension_semantics=semantics,
                                vmem_limit_bytes=VMEM_LIMIT_BYTES)


def _resident(shape):
    nd = len(shape)
    return pl.BlockSpec(shape, lambda *_: (0,) * nd, pipeline_mode=pl.Buffered(1))


def _rms(xf, g):
    ms = jnp.mean(xf * xf, axis=-1, keepdims=True)
    return xf * lax.rsqrt(ms + EPS) * g


def _proj_body(x_ref, pos_ref, g_ref, w1_ref, qn_ref, wqb_ref, kvn_ref, wkvb_ref, freq_ref,
               q_ref, k_ref, v_ref, u_ref, *, scale):
    h = _rms(x_ref[...], g_ref[...]).astype(BF16)
    p = jnp.dot(h, w1_ref[...], preferred_element_type=F32)
    o_kv, o_ka, o_kb, o_u = COL_Q, COL_KV, COL_KV + LANES, COL_KV + 2 * LANES
    cq = p[:, :o_kv]
    ckv = p[:, o_kv:o_ka]
    kpa = p[:, o_ka:o_kb]
    kpb = p[:, o_kb:o_u]
    u_ref[...] = p[:, o_u:]
    ang = pos_ref[...].astype(F32) * freq_ref[...]
    c = jnp.cos(ang)
    s = jnp.sin(ang)
    qq = jnp.dot(_rms(cq, qn_ref[...]).astype(BF16), wqb_ref[...], preferred_element_type=F32)
    kv = jnp.dot(_rms(ckv, kvn_ref[...]).astype(BF16), wkvb_ref[...], preferred_element_type=F32)
    kpe = (kpa * c + kpb * s).astype(BF16)
    swap0 = MLA_HEADS * HEAD_SLOT
    for hd in range(MLA_HEADS):
        a0 = hd * HEAD_SLOT
        q_ref[:, a0:a0 + LANES] = (qq[:, a0:a0 + LANES] * scale).astype(BF16)
        qa = qq[:, a0 + LANES:a0 + HEAD_SLOT]
        qb = qq[:, swap0 + hd * LANES:swap0 + (hd + 1) * LANES]
        q_ref[:, a0 + LANES:a0 + HEAD_SLOT] = ((qa * c + qb * s) * scale).astype(BF16)
        k_ref[:, a0:a0 + LANES] = kv[:, hd * LANES:(hd + 1) * LANES].astype(BF16)
        k_ref[:, a0 + LANES:a0 + HEAD_SLOT] = kpe
    v_ref[...] = kv[:, MLA_HEADS * QK_NOPE_DIM:].astype(BF16)


def _rope_slot_pair(w):
    half = QK_ROPE_DIM // 2
    pad = jnp.zeros(w.shape[:-1] + (LANES - QK_ROPE_DIM,), w.dtype)
    a = jnp.concatenate([w, pad], axis=-1)
    b = jnp.concatenate([-w[..., half:], w[..., :half], pad], axis=-1)
    return a, b


def _proj_call(x2d, pos2d, attn_norm, w_in, q_a_norm, w_q_b, kv_a_norm, w_kv_b, tm=256):
    T = x2d.shape[0]
    kpa, kpb = _rope_slot_pair(w_in[:, COL_KV:COL_KPE])
    w1 = jnp.concatenate([w_in[:, :COL_KV], kpa, kpb, w_in[:, COL_KPE:]], axis=1).astype(BF16)
    wq = w_q_b.reshape(Q_LORA_RANK, MLA_HEADS, QK_NOPE_DIM + QK_ROPE_DIM)
    qpa, qpb = _rope_slot_pair(wq[..., QK_NOPE_DIM:])
    wqb = jnp.concatenate([
        jnp.concatenate([wq[..., :QK_NOPE_DIM], qpa], axis=-1).reshape(Q_LORA_RANK, MLA_HEADS * HEAD_SLOT),
        qpb.reshape(Q_LORA_RANK, MLA_HEADS * LANES)], axis=1).astype(BF16)
    wkv = w_kv_b.reshape(KV_LORA_RANK, MLA_HEADS, QK_NOPE_DIM + V_HEAD_DIM)
    wkvb = jnp.concatenate([wkv[..., :QK_NOPE_DIM].reshape(KV_LORA_RANK, -1),
                            wkv[..., QK_NOPE_DIM:].reshape(KV_LORA_RANK, -1)], axis=1).astype(BF16)
    inv_freq = 1.0 / (ROPE_THETA ** (jnp.arange(0, QK_ROPE_DIM, 2, dtype=F32) / QK_ROPE_DIM))
    freq = jnp.concatenate([inv_freq, inv_freq, jnp.zeros((LANES - QK_ROPE_DIM,), F32)])[None, :]
    scale = (QK_NOPE_DIM + QK_ROPE_DIM) ** -0.5
    n1 = w1.shape[1]
    row = lambda w: pl.BlockSpec((tm, w), lambda i: (i, 0))
    return pl.pallas_call(
        functools.partial(_proj_body, scale=scale),
        grid=(T // tm,),
        in_specs=[row(D_MODEL), row(1), _resident((1, D_MODEL)), _resident((D_MODEL, n1)),
                  _resident((1, Q_LORA_RANK)), _resident(wqb.shape),
                  _resident((1, KV_LORA_RANK)), _resident(wkvb.shape), _resident((1, LANES))],
        out_specs=[row(MLA_HEADS * HEAD_SLOT), row(MLA_HEADS * HEAD_SLOT),
                   row(MLA_HEADS * V_HEAD_DIM), row(SSM_WIDTH)],
        out_shape=[jax.ShapeDtypeStruct((T, MLA_HEADS * HEAD_SLOT), BF16),
                   jax.ShapeDtypeStruct((T, MLA_HEADS * HEAD_SLOT), BF16),
                   jax.ShapeDtypeStruct((T, MLA_HEADS * V_HEAD_DIM), BF16),
                   jax.ShapeDtypeStruct((T, SSM_WIDTH), F32)],
        compiler_params=_cparams(("parallel",)),
        name="proj_mla_rope",
    )(x2d, pos2d, attn_norm[None, :], w1, q_a_norm[None, :], wqb, kv_a_norm[None, :], wkvb, freq)


def _attn_body(q_ref, k_ref, v_ref, o_ref, *, tq):
    i = pl.program_id(2)

    def step(j, carry, diagonal):
        off = pl.multiple_of(j * tq, tq)
        out = []
        for hd in range(ATTN_HEADS_PER_STEP):
            m, l, acc = carry[hd]
            q = q_ref[:, hd * HEAD_SLOT:(hd + 1) * HEAD_SLOT]
            kj = k_ref[pl.ds(off, tq), hd * HEAD_SLOT:(hd + 1) * HEAD_SLOT]
            vj = v_ref[pl.ds(off, tq), hd * V_HEAD_DIM:(hd + 1) * V_HEAD_DIM]
            s = lax.dot_general(q, kj, (((1,), (1,)), ((), ())), preferred_element_type=F32)
            if diagonal:
                rows = lax.broadcasted_iota(I32, s.shape, 0)
                cols = lax.broadcasted_iota(I32, s.shape, 1)
                s = jnp.where(rows >= cols, s, -jnp.inf)
            m_new = jnp.maximum(m, jnp.max(s, axis=-1, keepdims=True))
            alpha = jnp.exp(m - m_new)
            p = jnp.exp(s - m_new)
            l = alpha * l + jnp.sum(p, axis=-1, keepdims=True)
            acc = alpha * acc + jnp.dot(p.astype(BF16), vj, preferred_element_type=F32)
            out.append((m_new, l, acc))
        return tuple(out)

    init = tuple((jnp.full((tq, 1), -jnp.inf, F32), jnp.zeros((tq, 1), F32),
                  jnp.zeros((tq, V_HEAD_DIM), F32)) for _ in range(ATTN_HEADS_PER_STEP))
    carry = lax.fori_loop(0, i, lambda j, c: step(j, c, False), init)
    carry = step(i, carry, True)
    for hd in range(ATTN_HEADS_PER_STEP):
        _, l, acc = carry[hd]
        o_ref[:, hd * V_HEAD_DIM:(hd + 1) * V_HEAD_DIM] = acc / l


def _attn_call(q, k, v, batch, seq, tq=512):
    nq = seq // tq
    hp = ATTN_HEADS_PER_STEP
    return pl.pallas_call(
        functools.partial(_attn_body, tq=tq),
        grid=(batch, MLA_HEADS // hp, nq),
        in_specs=[pl.BlockSpec((tq, hp * HEAD_SLOT), lambda b, h, i: (b * nq + i, h)),
                  pl.BlockSpec((seq, hp * HEAD_SLOT), lambda b, h, i: (b, h)),
                  pl.BlockSpec((seq, hp * V_HEAD_DIM), lambda b, h, i: (b, h))],
        out_specs=pl.BlockSpec((tq, hp * V_HEAD_DIM), lambda b, h, i: (b * nq + i, h)),
        out_shape=jax.ShapeDtypeStruct((batch * seq, MLA_HEADS * V_HEAD_DIM), F32),
        compiler_params=_cparams(("parallel", "parallel", "arbitrary")),
        name="mla_attention",
    )(q, k, v)


def _s5_param_body(lr_ref, li_ref, ldt_ref, br_ref, bi_ref, abr_ref, abi_ref, bbr_ref, bbi_ref):
    lr = lr_ref[...]
    li = li_ref[...]
    dt = jnp.exp(ldt_ref[...])
    mag = jnp.exp(lr * dt)
    ab_re = mag * jnp.cos(li * dt)
    ab_im = mag * jnp.sin(li * dt)
    nr, ni = ab_re - 1.0, ab_im
    den = lr * lr + li * li
    coef_re = (nr * lr + ni * li) / den
    coef_im = (ni * lr - nr * li) / den
    abr_ref[...] = ab_re
    abi_ref[...] = ab_im
    for c in range(SSM_GROUP):
        br = br_ref[c]
        bi = bi_ref[c]
        bbr_ref[c] = coef_re * br - coef_im * bi
        bbi_ref[c] = coef_re * bi + coef_im * br


def _s5_param_call(lam_re, lam_im, log_dt, b_re, b_im):
    G, P, C = SSM_GROUPS, SSM_STATE, SSM_GROUP
    gp = jax.ShapeDtypeStruct((G, P), F32)
    cgp = jax.ShapeDtypeStruct((C, G, P), F32)
    return pl.pallas_call(
        _s5_param_body, out_shape=[gp, gp, cgp, cgp], name="s5_discretise",
    )(lam_re, lam_im, log_dt[:, None], b_re.transpose(2, 0, 1), b_im.transpose(2, 0, 1))


def _s5_body(u_ref, bre_ref, bim_ref, cre_ref, cim_ref, ar_ref, ai_ref, d_ref, wg_ref, bg_ref,
             o_ref, sre, sim, st_re, st_im, *, tt):
    nk = SSM_WIDTH // MXU_DIM
    cols_per_k = STATE_ELEMS // nk
    lane_blocks = cols_per_k // LANES
    c_idx = pl.program_id(1)

    @pl.when(c_idx == 0)
    def _():
        st_re[...] = jnp.zeros_like(st_re)
        st_im[...] = jnp.zeros_like(st_im)

    u = u_ref[...]
    ub = u.astype(BF16)
    for k in range(nk):
        uk = ub[:, k * MXU_DIM:(k + 1) * MXU_DIM]
        pr = jnp.dot(uk, bre_ref[k], preferred_element_type=F32)
        pi = jnp.dot(uk, bim_ref[k], preferred_element_type=F32)
        for m in range(lane_blocks):
            blk = k * lane_blocks + m
            s_, j_ = blk // STATE_SLABS, blk % STATE_SLABS
            sre[j_, pl.ds(s_, tt, stride=SUBLANES), :] = pr[:, m * LANES:(m + 1) * LANES]
            sim[j_, pl.ds(s_, tt, stride=SUBLANES), :] = pi[:, m * LANES:(m + 1) * LANES]

    ar = [ar_ref[j] for j in range(STATE_SLABS)]
    ai = [ai_ref[j] for j in range(STATE_SLABS)]

    def scan_step(t, carry):
        xr, xi = carry
        row = pl.multiple_of(t * SUBLANES, SUBLANES)
        nxr, nxi = [], []
        for j in range(STATE_SLABS):
            br = sre[j, pl.ds(row, SUBLANES), :]
            bi = sim[j, pl.ds(row, SUBLANES), :]
            r = ar[j] * xr[j] - ai[j] * xi[j] + br
            im = ar[j] * xi[j] + ai[j] * xr[j] + bi
            sre[j, pl.ds(row, SUBLANES), :] = r
            sim[j, pl.ds(row, SUBLANES), :] = im
            nxr.append(r)
            nxi.append(im)
        return tuple(nxr), tuple(nxi)

    x0 = (tuple(st_re[j] for j in range(STATE_SLABS)), tuple(st_im[j] for j in range(STATE_SLABS)))
    xr, xi = lax.fori_loop(0, tt, scan_step, x0, unroll=8)
    for j in range(STATE_SLABS):
        st_re[j] = xr[j]
        st_im[j] = xi[j]

    ys = []
    for k in range(nk):
        xre, xim = [], []
        for m in range(lane_blocks):
            blk = k * lane_blocks + m
            s_, j_ = blk // STATE_SLABS, blk % STATE_SLABS
            xre.append(sre[j_, pl.ds(s_, tt, stride=SUBLANES), :])
            xim.append(sim[j_, pl.ds(s_, tt, stride=SUBLANES), :])
        xre = jnp.concatenate(xre, axis=1).astype(BF16)
        xim = jnp.concatenate(xim, axis=1).astype(BF16)
        ys.append(jnp.dot(xre, cre_ref[k], preferred_element_type=F32)
                  - jnp.dot(xim, cim_ref[k], preferred_element_type=F32))
    y = jnp.concatenate(ys, axis=1) + d_ref[...] * u
    y = jax.nn.gelu(y)
    gate = jax.nn.sigmoid(jnp.dot(y.astype(BF16), wg_ref[...], preferred_element_type=F32) + bg_ref[...])
    o_ref[...] = y * gate


def _block_diag(blocks):
    nk, ng, r, c = blocks.shape
    eye = jnp.eye(ng, dtype=blocks.dtype)
    out = blocks[:, :, :, None, :] * eye[None, :, None, :, None]
    return out.reshape(nk, ng * r, ng * c)


def _s5_call(u, ab_re, ab_im, bb_re, bb_im, c_re, c_im, d_skip, w_glu, b_glu, batch, seq, tt=256):
    G, P, C = SSM_GROUPS, SSM_STATE, SSM_GROUP
    nk = SSM_WIDTH // MXU_DIM
    gk = G // nk
    def bmat(bb):
        blocks = bb.transpose(1, 0, 2).reshape(nk, gk, C, P)
        return _block_diag(blocks).astype(BF16)
    def cmat(cc):
        blocks = cc.transpose(0, 2, 1).reshape(nk, gk, P, C)
        return _block_diag(blocks).astype(BF16)
    def slab(a):
        return a.reshape(SUBLANES, STATE_SLABS, LANES).transpose(1, 0, 2)
    nc = seq // tt
    row = pl.BlockSpec((tt, SSM_WIDTH), lambda b, c: (b * nc + c, 0))
    return pl.pallas_call(
        functools.partial(_s5_body, tt=tt),
        grid=(batch, nc),
        in_specs=[row, _resident((nk, MXU_DIM, STATE_ELEMS // nk)), _resident((nk, MXU_DIM, STATE_ELEMS // nk)),
                  _resident((nk, STATE_ELEMS // nk, MXU_DIM)), _resident((nk, STATE_ELEMS // nk, MXU_DIM)),
                  _resident((STATE_SLABS, SUBLANES, LANES)), _resident((STATE_SLABS, SUBLANES, LANES)),
                  _resident((1, SSM_WIDTH)), _resident((SSM_WIDTH, SSM_WIDTH)), _resident((1, SSM_WIDTH))],
        out_specs=row,
        out_shape=jax.ShapeDtypeStruct((batch * seq, SSM_WIDTH), F32),
        scratch_shapes=[pltpu.VMEM((STATE_SLABS, tt * SUBLANES, LANES), F32),
                        pltpu.VMEM((STATE_SLABS, tt * SUBLANES, LANES), F32),
                        pltpu.VMEM((STATE_SLABS, SUBLANES, LANES), F32),
                        pltpu.VMEM((STATE_SLABS, SUBLANES, LANES), F32)],
        compiler_params=_cparams(("parallel", "arbitrary")),
        name="s5_scan_glu",
    )(u, bmat(bb_re), bmat(bb_im), cmat(c_re), cmat(c_im), slab(ab_re), slab(ab_im),
      d_skip[None, :], w_glu.astype(BF16), b_glu[None, :])


def _mix_body(x_ref, ao_ref, so_ref, ga_ref, gs_ref, woa_ref, wos_ref, gx_ref, wxq_ref,
              x1_ref, qx_ref, *, xscale):
    a = _rms(ao_ref[...], ga_ref[...]).astype(BF16)
    s = _rms(so_ref[...], gs_ref[...]).astype(BF16)
    x1 = (x_ref[...] + jnp.dot(a, woa_ref[...], preferred_element_type=F32)
          + jnp.dot(s, wos_ref[...], preferred_element_type=F32))
    x1_ref[...] = x1
    hq = _rms(x1, gx_ref[...]).astype(BF16)
    qx_ref[...] = (jnp.dot(hq, wxq_ref[...], preferred_element_type=F32) * xscale).astype(BF16)


def _mix_call(x2d, attn_o, ssm_o, g_attn, g_ssm, w_out, g_x, w_xq, tm=256):
    T = x2d.shape[0]
    na = MLA_HEADS * V_HEAD_DIM
    row = lambda w: pl.BlockSpec((tm, w), lambda i: (i, 0))
    return pl.pallas_call(
        functools.partial(_mix_body, xscale=XATTN_HEAD_DIM ** -0.5),
        grid=(T // tm,),
        in_specs=[row(D_MODEL), row(na), row(SSM_WIDTH), _resident((1, na)), _resident((1, SSM_WIDTH)),
                  _resident((na, D_MODEL)), _resident((SSM_WIDTH, D_MODEL)),
                  _resident((1, D_MODEL)), _resident((D_MODEL, D_MODEL))],
        out_specs=[row(D_MODEL), row(D_MODEL)],
        out_shape=[jax.ShapeDtypeStruct((T, D_MODEL), F32), jax.ShapeDtypeStruct((T, D_MODEL), BF16)],
        compiler_params=_cparams(("parallel",)),
        name="mix_out_proj",
    )(x2d, attn_o, ssm_o, g_attn[None, :], g_ssm[None, :], w_out[:na].astype(BF16),
      w_out[na:].astype(BF16), g_x[None, :], w_xq.astype(BF16))


def _memkv_body(m_ref, g_ref, wk_ref, wv_ref, k_ref, v_ref):
    h = _rms(m_ref[...], g_ref[...]).astype(BF16)
    k_ref[...] = jnp.dot(h, wk_ref[...], preferred_element_type=F32).astype(BF16)
    v_ref[...] = jnp.dot(h, wv_ref[...], preferred_element_type=F32).astype(BF16)


def _memkv_call(mem2d, g_mem, w_xk, w_xv, tm=256):
    R = mem2d.shape[0]
    row = pl.BlockSpec((tm, D_MODEL), lambda i: (i, 0))
    return pl.pallas_call(
        _memkv_body,
        grid=(R // tm,),
        in_specs=[row, _resident((1, D_MODEL)), _resident((D_MODEL, D_MODEL)), _resident((D_MODEL, D_MODEL))],
        out_specs=[row, row],
        out_shape=[jax.ShapeDtypeStruct((R, D_MODEL), BF16)] * 2,
        compiler_params=_cparams(("parallel",)),
        name="mem_kv_proj",
    )(mem2d, g_mem[None, :], w_xk.astype(BF16), w_xv.astype(BF16))


def _pack_bf16_pairs(xf):
    c = xf.shape[1] // 2
    bits = pltpu.bitcast(xf.astype(BF16).astype(F32), jnp.uint32)
    return lax.shift_right_logical(bits[:, :c], jnp.uint32(16)) | bits[:, c:]


def _unpack_bf16_pairs(w):
    lo = pltpu.bitcast(lax.shift_left(w, jnp.uint32(16)), F32).astype(BF16)
    hi = pltpu.bitcast(w & jnp.uint32(0xFFFF0000), F32).astype(BF16)
    return lo, hi


def _xattn_body(q_ref, k_ref, v_ref, x1_ref, wo_ref, gf_ref, x2_ref, xn_ref, xp_ref):
    outs = []
    for h in range(XATTN_HEADS):
        sl = slice(h * XATTN_HEAD_DIM, (h + 1) * XATTN_HEAD_DIM)
        s = lax.dot_general(q_ref[:, sl], k_ref[:, sl], (((1,), (1,)), ((), ())),
                            preferred_element_type=F32)
        p = jnp.exp(s - jnp.max(s, axis=-1, keepdims=True))
        denom = jnp.sum(p, axis=-1, keepdims=True)
        o = jnp.dot(p.astype(BF16), v_ref[:, sl], preferred_element_type=F32) / denom
        outs.append(o.astype(BF16))
    o = jnp.concatenate(outs, axis=1)
    x2 = x1_ref[...] + jnp.dot(o, wo_ref[...], preferred_element_type=F32)
    x2_ref[...] = x2
    xn = _rms(x2, gf_ref[...])
    xn_ref[...] = xn
    packed = _pack_bf16_pairs(xn)
    tq = packed.shape[0]
    for s in range(SUBLANES):
        xp_ref[pl.ds(s, tq, stride=SUBLANES), :] = packed[:, s * LANES:(s + 1) * LANES]


def _xattn_call(qx, kx, vx, x1, w_xo, g_ffn, batch, seq, tq=256):
    nq = seq // tq
    row = pl.BlockSpec((tq, D_MODEL), lambda b, i: (b * nq + i, 0))
    half = pl.BlockSpec((tq * SUBLANES, LANES), lambda b, i: (b * nq + i, 0))
    memb = pl.BlockSpec((MEM_LEN, D_MODEL), lambda b, i: (b, 0))
    return pl.pallas_call(
        _xattn_body,
        grid=(batch, nq),
        in_specs=[row, memb, memb, row, _resident((D_MODEL, D_MODEL)), _resident((1, D_MODEL))],
        out_specs=[row, row, half],
        out_shape=[jax.ShapeDtypeStruct((batch * seq, D_MODEL), F32),
                   jax.ShapeDtypeStruct((batch * seq, D_MODEL), F32),
                   jax.ShapeDtypeStruct((batch * seq * SUBLANES, LANES), jnp.uint32)],
        compiler_params=_cparams(("parallel", "parallel")),
        name="cross_attention",
    )(qx, kx, vx, x1, w_xo.astype(BF16), g_ffn[None, :])


def _router_body(xn_ref, wr_ref, br_ref, idx_ref, gate_ref, rank_ref, cnt_ref, carry, *, tm):
    @pl.when(pl.program_id(0) == 0)
    def _():
        carry[...] = jnp.zeros_like(carry)

    logits = jnp.dot(xn_ref[...], wr_ref[...], preferred_element_type=F32,
                     precision=lax.Precision.HIGHEST) + br_ref[...]
    lane = lax.broadcasted_iota(I32, logits.shape, 1)
    work = logits
    sels, vals, ids = [], [], []
    for _ in range(TOP_K):
        mx = jnp.max(work, axis=-1, keepdims=True)
        am = jnp.min(jnp.where(work == mx, lane, N_EXPERTS), axis=-1, keepdims=True)
        sel = lane == am
        sels.append(sel)
        vals.append(mx)
        ids.append(am)
        work = jnp.where(sel, -jnp.inf, work)
    es = [jnp.exp(v - vals[0]) for v in vals]
    denom = es[0] + es[1] + es[2] + es[3]
    member = jnp.zeros(logits.shape, F32)
    for sel in sels:
        member = member + jnp.where(sel, 1.0, 0.0)
    r_i = lax.broadcasted_iota(I32, (tm, tm), 0)
    c_i = lax.broadcasted_iota(I32, (tm, tm), 1)
    tri = jnp.where(r_i > c_i, 1.0, 0.0).astype(BF16)
    before = jnp.dot(tri, member.astype(BF16), preferred_element_type=F32) + carry[...]
    out_lane = lax.broadcasted_iota(I32, (tm, LANES), 1)
    idx_o = jnp.zeros((tm, LANES), I32)
    gate_o = jnp.zeros((tm, LANES), F32)
    rank_o = jnp.zeros((tm, LANES), I32)
    for k in range(TOP_K):
        rk = jnp.sum(jnp.where(sels[k], before, 0.0), axis=-1, keepdims=True).astype(I32)
        idx_o = jnp.where(out_lane == k, ids[k], idx_o)
        gate_o = jnp.where(out_lane == k, es[k] / denom, gate_o)
        rank_o = jnp.where(out_lane == k, rk, rank_o)
    idx_ref[...] = idx_o
    gate_ref[...] = gate_o
    rank_ref[...] = rank_o
    carry[...] = carry[...] + jnp.sum(member, axis=0, keepdims=True)
    cnt_ref[...] = carry[...]


def _router_call(xn, w_router, b_router, tm=256):
    T = xn.shape[0]
    wide = pl.BlockSpec((tm, LANES), lambda i: (i, 0))
    return pl.pallas_call(
        functools.partial(_router_body, tm=tm),
        grid=(T // tm,),
        in_specs=[pl.BlockSpec((tm, D_MODEL), lambda i: (i, 0)),
                  _resident((D_MODEL, N_EXPERTS)), _resident((1, N_EXPERTS))],
        out_specs=[wide, wide, wide, pl.BlockSpec((1, N_EXPERTS), lambda i: (0, 0))],
        out_shape=[jax.ShapeDtypeStruct((T, LANES), I32), jax.ShapeDtypeStruct((T, LANES), F32),
                   jax.ShapeDtypeStruct((T, LANES), I32), jax.ShapeDtypeStruct((1, N_EXPERTS), F32)],
        scratch_shapes=[pltpu.VMEM((1, N_EXPERTS), F32)],
        compiler_params=_cparams(("arbitrary",)),
        name="moe_router",
    )(xn, w_router, b_router[None, :])


def _dispatch_body(fill_ref, dest_ref, xp_ref, xs_hbm, zbuf, sems, *, tm):
    @pl.when(pl.program_id(0) == 0)
    def _():
        zbuf[...] = jnp.zeros_like(zbuf)

        def zero_copy(e):
            dst = pl.multiple_of(fill_ref[e] * PACKED_TILE_ROWS, PACKED_TILE_ROWS)
            return pltpu.make_async_copy(zbuf, xs_hbm.at[pl.ds(dst, PACKED_TILE_ROWS)], sems.at[0])

        def start(e, c):
            @pl.when(fill_ref[e] >= 0)
            def _():
                zero_copy(e).start()
            return c

        def wait(e, c):
            @pl.when(fill_ref[e] >= 0)
            def _():
                zero_copy(e).wait()
            return c

        lax.fori_loop(0, 2 * N_EXPERTS, start, 0)
        lax.fori_loop(0, 2 * N_EXPERTS, wait, 0)

    def row_copy(r, dst):
        src = pl.multiple_of(r * SUBLANES, SUBLANES)
        dst = pl.multiple_of(dst * SUBLANES, SUBLANES)
        return pltpu.make_async_copy(xp_ref.at[pl.ds(src, SUBLANES)], xs_hbm.at[pl.ds(dst, SUBLANES)],
                                     sems.at[1])

    def issue(r, c):
        for k in range(TOP_K):
            row_copy(r, dest_ref[0, 0, r * TOP_K + k]).start(priority=k % 2)
        return c

    lax.fori_loop(0, tm, issue, 0)

    def drain(r, c):
        for k in range(TOP_K):
            row_copy(0, 0).wait()
        return c

    lax.fori_loop(0, tm, drain, 0)


def _dispatch_call(fill_tile, dest3, xnp, n_slots):
    tm = COMBINE_ROWS
    T = xnp.shape[0] // SUBLANES
    grid_spec = pltpu.PrefetchScalarGridSpec(
        num_scalar_prefetch=1,
        grid=(T // tm,),
        in_specs=[pl.BlockSpec((1, 1, tm * TOP_K), lambda i, fl: (i, 0, 0), memory_space=pltpu.SMEM),
                  pl.BlockSpec((tm * SUBLANES, LANES), lambda i, fl: (i, 0))],
        out_specs=pl.BlockSpec(memory_space=pl.ANY),
        scratch_shapes=[pltpu.VMEM((PACKED_TILE_ROWS, LANES), jnp.uint32), pltpu.SemaphoreType.DMA((2,))],
    )
    return pl.pallas_call(
        functools.partial(_dispatch_body, tm=tm),
        grid_spec=grid_spec,
        out_shape=jax.ShapeDtypeStruct((n_slots * SUBLANES, LANES), jnp.uint32),
        compiler_params=_cparams(("arbitrary",)),
        name="moe_dispatch",
    )(fill_tile, dest3, xnp)


def _expert_body(ue_ref, ur_ref, us_ref, tail_ref, xs_hbm, wg_ref, wl_ref, bg_ref, bl_ref,
                 wd_ref, bd_ref, out_hbm, raw, acc, wg_bf, wl_bf, wd_bf, sems):
    u = pl.program_id(0)
    j = pl.program_id(1)
    rows = ur_ref[u]
    s0 = us_ref[u]
    nsub = (rows + (MOE_ROW_TILE - 1)) // MOE_ROW_TILE
    half = D_MODEL // 2

    def in_copy(first_slot, r):
        off = pl.multiple_of(r * PACKED_TILE_ROWS, PACKED_TILE_ROWS)
        src = pl.multiple_of((first_slot + r * MOE_ROW_TILE) * SUBLANES, PACKED_TILE_ROWS)
        return pltpu.make_async_copy(xs_hbm.at[pl.ds(src, PACKED_TILE_ROWS)],
                                     raw.at[pl.ds(off, PACKED_TILE_ROWS)], sems.at[0])

    def fetch_rows(first_slot, n_tiles):
        def start(r, c):
            in_copy(first_slot, r).start()
            return c

        lax.fori_loop(0, n_tiles, start, 0)

    @pl.when(jnp.logical_and(j == 0, rows > 0))
    def _():
        @pl.when(u == 0)
        def _():
            fetch_rows(s0, nsub)

        bd = jnp.broadcast_to(bd_ref[0], (MOE_ROW_TILE, D_MODEL))

        def init(r, c):
            off = pl.multiple_of(r * MOE_ROW_TILE, MOE_ROW_TILE)
            acc[pl.ds(off, MOE_ROW_TILE), :] = bd
            return c

        def wait(r, c):
            in_copy(s0, r).wait()
            return c

        lax.fori_loop(0, nsub, init, 0)
        lax.fori_loop(0, nsub, wait, 0)

    bg = bg_ref[0]
    bl = bl_ref[0]

    last_step = j == MOE_NJ - 1
    pair = 2 * MOE_ROW_TILE
    npair = nsub // 2
    odd = nsub - 2 * npair

    def up(off, n, wg, wl):
        base = off * SUBLANES
        parts = [_unpack_bf16_pairs(raw[pl.ds(base + s, n, stride=SUBLANES), :]) for s in range(SUBLANES)]
        lo = jnp.concatenate([p[0] for p in parts], axis=1)
        hi = jnp.concatenate([p[1] for p in parts], axis=1)
        g = (jnp.dot(lo, wg[:half], preferred_element_type=F32)
             + jnp.dot(hi, wg[half:], preferred_element_type=F32) + bg)
        l = (jnp.dot(lo, wl[:half], preferred_element_type=F32)
             + jnp.dot(hi, wl[half:], preferred_element_type=F32) + bl)
        glu = jnp.minimum(g, SWIGLU_LIMIT)
        lin = jnp.clip(l, -SWIGLU_LIMIT, SWIGLU_LIMIT)
        return (glu * jax.nn.sigmoid(SWIGLU_ALPHA * glu) * (lin + 1.0)).astype(BF16)

    def down(off, n, act, wd):
        acc[pl.ds(off, n), :] += jnp.dot(act, wd, preferred_element_type=F32)

    def out_copy(off, n):
        dst = pl.multiple_of(s0 + off, MOE_ROW_TILE)
        return pltpu.make_async_copy(acc.at[pl.ds(off, n)], out_hbm.at[pl.ds(dst, n)], sems.at[2])

    def send(off, n):
        @pl.when(last_step)
        def _():
            out_copy(off, n).start()

    def cast_weights():
        wg = wg_ref[0].astype(BF16)
        wl = wl_ref[0].astype(BF16)
        wd = wd_ref[0].astype(BF16)
        wg_bf[...] = wg
        wl_bf[...] = wl
        wd_bf[...] = wd
        return wg, wl

    @pl.when(npair > 0)
    def _():
        wg, wl = cast_weights()
        act0 = up(0, pair, wg, wl)

        def body(r, act):
            off = pl.multiple_of(r * pair, pair)
            nxt = pl.multiple_of(off + pair, pair)
            down(off, pair, act, wd_bf[...])
            act = up(nxt, pair, wg_bf[...], wl_bf[...])
            send(off, pair)
            return act

        act_last = lax.fori_loop(0, npair - 1, body, act0)
        off = pl.multiple_of((npair - 1) * pair, pair)
        down(off, pair, act_last, wd_bf[...])
        send(off, pair)

    @pl.when(odd > 0)
    def _():
        @pl.when(npair == 0)
        def _():
            cast_weights()

        off = pl.multiple_of(npair * pair, pair)
        down(off, MOE_ROW_TILE, up(off, MOE_ROW_TILE, wg_bf[...], wl_bf[...]), wd_bf[...])
        send(off, MOE_ROW_TILE)

    nxt_u = jnp.minimum(u + 1, pl.num_programs(0) - 1)
    nxt_rows = jnp.where(u + 1 < pl.num_programs(0), ur_ref[nxt_u], 0)

    @pl.when(jnp.logical_and(last_step, nxt_rows > 0))
    def _():
        fetch_rows(us_ref[nxt_u], (nxt_rows + (MOE_ROW_TILE - 1)) // MOE_ROW_TILE)

    @pl.when(last_step)
    def _():
        def wait_pair(r, c):
            out_copy(0, pair).wait()
            return c

        lax.fori_loop(0, npair, wait_pair, 0)

        @pl.when(odd > 0)
        def _():
            out_copy(0, MOE_ROW_TILE).wait()


    @pl.when(jnp.logical_and(u == pl.num_programs(0) - 1, j == MOE_NJ - 1))
    def _():
        acc[pl.ds(0, MOE_ROW_TILE), :] = jnp.zeros((MOE_ROW_TILE, D_MODEL), F32)
        first = tail_ref[0]
        count = tail_ref[1]

        def zero_copy(t):
            dst = pl.multiple_of((first + t) * MOE_ROW_TILE, MOE_ROW_TILE)
            return pltpu.make_async_copy(acc.at[pl.ds(0, MOE_ROW_TILE)],
                                         out_hbm.at[pl.ds(dst, MOE_ROW_TILE)], sems.at[2])

        def start(t, c):
            zero_copy(t).start()
            return c

        def wait(t, c):
            zero_copy(t).wait()
            return c

        lax.fori_loop(0, count, start, 0)
        lax.fori_loop(0, count, wait, 0)


def _expert_call(unit_e, unit_rows, unit_s0, tail, xs, w_up, b_up, w_down, b_down, n_slots, n_units):
    fc = MOE_FF_CHUNK
    grid_spec = pltpu.PrefetchScalarGridSpec(
        num_scalar_prefetch=4,
        grid=(n_units, MOE_NJ),
        in_specs=[
            pl.BlockSpec(memory_space=pl.ANY),
            pl.BlockSpec((1, D_MODEL, fc), lambda u, j, ue, ur, us, tl: (ue[u], 0, j)),
            pl.BlockSpec((1, D_MODEL, fc), lambda u, j, ue, ur, us, tl: (ue[u], 0, MOE_NJ + j)),
            pl.BlockSpec((1, 1, fc), lambda u, j, ue, ur, us, tl: (ue[u], 0, j)),
            pl.BlockSpec((1, 1, fc), lambda u, j, ue, ur, us, tl: (ue[u], 0, MOE_NJ + j)),
            pl.BlockSpec((1, fc, D_MODEL), lambda u, j, ue, ur, us, tl: (ue[u], j, 0)),
            pl.BlockSpec((1, 1, D_MODEL), lambda u, j, ue, ur, us, tl: (ue[u], 0, 0)),
        ],
        out_specs=pl.BlockSpec(memory_space=pl.ANY),
        scratch_shapes=[
            pltpu.VMEM((MOE_UNIT_ROWS * SUBLANES, LANES), jnp.uint32),
            pltpu.VMEM((MOE_UNIT_ROWS, D_MODEL), F32),
            pltpu.VMEM((D_MODEL, fc), BF16),
            pltpu.VMEM((D_MODEL, fc), BF16),
            pltpu.VMEM((fc, D_MODEL), BF16),
            pltpu.SemaphoreType.DMA((3,)),
        ],
    )
    return pl.pallas_call(
        _expert_body,
        grid_spec=grid_spec,
        out_shape=jax.ShapeDtypeStruct((n_slots, D_MODEL), F32),
        compiler_params=_cparams(("arbitrary", "arbitrary")),
        name="moe_experts",
    )(unit_e, unit_rows, unit_s0, tail, xs, w_up, w_up, b_up[:, None, :], b_up[:, None, :],
      w_down, b_down[:, None, :])


def _combine_body(dest_ref, next_ref, gate_ref, x2_ref, gf_ref, out_hbm, o_ref, buf, sems, *, tm):
    i = pl.program_id(0)
    n = pl.num_programs(0)
    slot = lax.rem(i, 2)

    def row_copy(sl, r, k, src):
        return pltpu.make_async_copy(out_hbm.at[pl.ds(src, 1)], buf.at[sl, k, pl.ds(r, 1)], sems.at[sl])

    def issue_tile(table, sl):
        def issue(r, c):
            for k in range(TOP_K):
                row_copy(sl, r, k, table[0, 0, r * TOP_K + k]).start(priority=k % 2)
            return c

        lax.fori_loop(0, tm, issue, 0)

    @pl.when(i == 0)
    def _():
        issue_tile(dest_ref, 0)

    @pl.when(i + 1 < n)
    def _():
        issue_tile(next_ref, 1 - slot)

    def drain(r, c):
        for k in range(TOP_K):
            row_copy(slot, 0, k, 0).wait()
        return c

    lax.fori_loop(0, tm, drain, 0)
    gates = gate_ref[...]
    y = gates[:, 0:1] * buf[slot, 0]
    for k in range(1, TOP_K):
        y = y + gates[:, k:k + 1] * buf[slot, k]
    o_ref[...] = _rms(x2_ref[...] + y, gf_ref[...])


def _combine_call(dest3, gates, x2, final_norm, expert_out):
    tm = COMBINE_ROWS
    T = x2.shape[0]
    n = T // tm
    row = pl.BlockSpec((tm, D_MODEL), lambda i: (i, 0))
    table = lambda f: pl.BlockSpec((1, 1, tm * TOP_K), f, memory_space=pltpu.SMEM)
    return pl.pallas_call(
        functools.partial(_combine_body, tm=tm),
        grid=(n,),
        in_specs=[table(lambda i: (i, 0, 0)), table(lambda i: (jnp.minimum(i + 1, n - 1), 0, 0)),
                  pl.BlockSpec((tm, LANES), lambda i: (i, 0)), row, _resident((1, D_MODEL)),
                  pl.BlockSpec(memory_space=pl.ANY)],
        out_specs=row,
        out_shape=jax.ShapeDtypeStruct((T, D_MODEL), F32),
        scratch_shapes=[pltpu.VMEM((2, TOP_K, tm, D_MODEL), F32), pltpu.SemaphoreType.DMA((2,))],
        compiler_params=_cparams(("arbitrary",)),
        name="moe_combine_norm",
    )(dest3, dest3, gates, x2, final_norm[None, :], expert_out)


def _routing_tables(idx, rank, counts, n_tokens):
    tk = n_tokens * TOP_K
    padded = (counts + MOE_ROW_TILE - 1) // MOE_ROW_TILE * MOE_ROW_TILE
    pad_end = jnp.cumsum(padded)
    pad_start = pad_end - padded
    onehot = idx[:, :, None] == jnp.arange(N_EXPERTS, dtype=I32)[None, None, :]
    dest = jnp.sum(jnp.where(onehot, pad_start[None, None, :], 0), axis=-1) + rank
    n_slots = (tk // MOE_ROW_TILE + N_EXPERTS) * MOE_ROW_TILE
    n_tiles = n_slots // MOE_ROW_TILE
    part_tile = jnp.where(counts % MOE_ROW_TILE != 0, pad_end // MOE_ROW_TILE - 1, -1)
    spare = pad_end[-1] // MOE_ROW_TILE + jnp.arange(N_EXPERTS, dtype=I32)
    fill_tile = jnp.concatenate([part_tile, jnp.where(spare < n_tiles, spare, -1)]).astype(I32)
    n_units = N_EXPERTS + tk // MOE_UNIT_ROWS
    units_per_e = (counts + MOE_UNIT_ROWS - 1) // MOE_UNIT_ROWS
    unit_end = jnp.cumsum(units_per_e)
    n_active = unit_end[-1]
    uid = jnp.arange(n_units, dtype=I32)
    ue = jnp.minimum(jnp.searchsorted(unit_end, uid, side='right'), N_EXPERTS - 1).astype(I32)
    k_in_e = uid - (unit_end - units_per_e)[ue]
    rows = jnp.clip(counts[ue] - k_in_e * MOE_UNIT_ROWS, 0, MOE_UNIT_ROWS)
    active = uid < n_active
    last_e = ue[jnp.maximum(n_active - 1, 0)]
    unit_e = jnp.where(active, ue, last_e).astype(I32)
    unit_rows = jnp.where(active, rows, 0).astype(I32)
    unit_s0 = jnp.where(active, pad_start[ue] + k_in_e * MOE_UNIT_ROWS, 0).astype(I32)
    used_tiles = pad_end[-1] // MOE_ROW_TILE
    tail = jnp.stack([used_tiles, n_slots // MOE_ROW_TILE - used_tiles]).astype(I32)
    return dest, fill_tile, unit_e, unit_rows, unit_s0, tail, n_slots, n_units


def kernel(x, mem, positions, attn_norm, w_in, q_a_norm, w_q_b, kv_a_norm, w_kv_b, ssm_lambda_re, ssm_lambda_im, ssm_log_dt, ssm_b_re, ssm_b_im, ssm_c_re, ssm_c_im, ssm_d, w_glu, b_glu, mix_norm_attn, mix_norm_ssm, w_out, xattn_norm, mem_norm, w_xq, w_xk, w_xv, w_xo, ffn_norm, w_router, b_router, w_up, b_up, w_down, b_down, final_norm):
    B, S, D = x.shape
    T = B * S
    assert w_in.shape[0] == 1, "single-layer block"
    l = 0
    xt = x.reshape(T, D)
    pos2d = positions.reshape(T, 1)
    mem2d = mem.reshape(B * MEM_LEN, D)
    q, k, v, u = _proj_call(xt, pos2d, attn_norm[l], w_in[l], q_a_norm[l], w_q_b[l],
                            kv_a_norm[l], w_kv_b[l])
    attn_o = _attn_call(q, k, v, B, S)
    ab_re, ab_im, bb_re, bb_im = _s5_param_call(ssm_lambda_re[l], ssm_lambda_im[l], ssm_log_dt[l],
                                                ssm_b_re[l], ssm_b_im[l])
    ssm_o = _s5_call(u, ab_re, ab_im, bb_re, bb_im, ssm_c_re[l], ssm_c_im[l], ssm_d[l],
                     w_glu[l], b_glu[l], B, S)
    x1, qx = _mix_call(xt, attn_o, ssm_o, mix_norm_attn[l], mix_norm_ssm[l], w_out[l],
                       xattn_norm[l], w_xq[l])
    kx, vx = _memkv_call(mem2d, mem_norm[l], w_xk[l], w_xv[l])
    x2, xn, xnp = _xattn_call(qx, kx, vx, x1, w_xo[l], ffn_norm[l], B, S)
    idx_w, gate_w, rank_w, counts_f = _router_call(xn, w_router[l], b_router[l])
    counts = counts_f[0].astype(I32)
    dest, fill_tile, unit_e, unit_rows, unit_s0, tail, n_slots, n_units = _routing_tables(
        idx_w[:, :TOP_K], rank_w[:, :TOP_K], counts, T)
    dest3 = dest.reshape(-1, 1, COMBINE_ROWS * TOP_K)
    xs = _dispatch_call(fill_tile, dest3, xnp, n_slots)
    expert_out = _expert_call(unit_e, unit_rows, unit_s0, tail, xs,
                              w_up[l], b_up[l], w_down[l], b_down[l], n_slots, n_units)
    out = _combine_call(dest3, gate_w, x2, final_norm, expert_out)
    return out.reshape(B, S, D)
```

```python
import functools

import jax
import jax.numpy as jnp
from jax import lax
from jax.experimental import pallas as pl
from jax.experimental.pallas import tpu as pltpu

F32 = jnp.float32
BF16 = jnp.bfloat16
I32 = jnp.int32

D_MODEL = 2048
MEM_LEN = 256
MLA_HEADS = 8
QK_NOPE_DIM = 128
QK_ROPE_DIM = 64
V_HEAD_DIM = 128
Q_LORA_RANK = 384
KV_LORA_RANK = 256
ROPE_THETA = 10000.0
SSM_WIDTH = 1024
SSM_GROUP = 16
SSM_GROUPS = 64
SSM_STATE = 64
XATTN_HEADS = 4
XATTN_HEAD_DIM = D_MODEL // XATTN_HEADS
N_EXPERTS = 32
TOP_K = 4
D_FF = D_MODEL
SWIGLU_LIMIT = 7.0
SWIGLU_ALPHA = 1.702
EPS = 1e-6
COL_Q = Q_LORA_RANK
COL_KV = COL_Q + KV_LORA_RANK
COL_KPE = COL_KV + QK_ROPE_DIM

LANES = 128
SUBLANES = 8
MXU_DIM = 256
VMEM_LIMIT_BYTES = 56 * 1024 * 1024

HEAD_SLOT = 2 * LANES
ATTN_HEADS_PER_STEP = 4
STATE_ELEMS = SSM_GROUPS * SSM_STATE
STATE_SLABS = STATE_ELEMS // (SUBLANES * LANES)

MOE_ROW_TILE = 256
MOE_UNIT_ROWS = 1280
MOE_FF_CHUNK = 512
MOE_NJ = D_FF // MOE_FF_CHUNK
COMBINE_ROWS = 256
assert D_MODEL // 2 == SUBLANES * LANES
PACKED_TILE_ROWS = MOE_ROW_TILE * SUBLANES


def _cparams(semantics):
    return pltpu.CompilerParams(dimension_semantics=semantics,
                                vmem_limit_bytes=VMEM_LIMIT_BYTES)


def _resident(shape):
    nd = len(shape)
    return pl.BlockSpec(shape, lambda *_: (0,) * nd, pipeline_mode=pl.Buffered(1))


def _rms(xf, g):
    ms = jnp.mean(xf * xf, axis=-1, keepdims=True)
    return xf * lax.rsqrt(ms + EPS) * g


def _proj_body(x_ref, pos_ref, g_ref, w1_ref, qn_ref, wqb_ref, kvn_ref, wkvb_ref, freq_ref,
               q_ref, k_ref, v_ref, u_ref, *, scale):
    h = _rms(x_ref[...], g_ref[...]).astype(BF16)
    p = jnp.dot(h, w1_ref[...], preferred_element_type=F32)
    o_kv, o_ka, o_kb, o_u = COL_Q, COL_KV, COL_KV + LANES, COL_KV + 2 * LANES
    cq = p[:, :o_kv]
    ckv = p[:, o_kv:o_ka]
    kpa = p[:, o_ka:o_kb]
    kpb = p[:, o_kb:o_u]
    u_ref[...] = p[:, o_u:]
    ang = pos_ref[...].astype(F32) * freq_ref[...]
    c = jnp.cos(ang)
    s = jnp.sin(ang)
    qq = jnp.dot(_rms(cq, qn_ref[...]).astype(BF16), wqb_ref[...], preferred_element_type=F32)
    kv = jnp.dot(_rms(ckv, kvn_ref[...]).astype(BF16), wkvb_ref[...], preferred_element_type=F32)
    kpe = (kpa * c + kpb * s).astype(BF16)
    swap0 = MLA_HEADS * HEAD_SLOT
    for hd in range(MLA_HEADS):
        a0 = hd * HEAD_SLOT
        q_ref[:, a0:a0 + LANES] = (qq[:, a0:a0 + LANES] * scale).astype(BF16)
        qa = qq[:, a0 + LANES:a0 + HEAD_SLOT]
        qb = qq[:, swap0 + hd * LANES:swap0 + (hd + 1) * LANES]
        q_ref[:, a0 + LANES:a0 + HEAD_SLOT] = ((qa * c + qb * s) * scale).astype(BF16)
        k_ref[:, a0:a0 + LANES] = kv[:, hd * LANES:(hd + 1) * LANES].astype(BF16)
        k_ref[:, a0 + LANES:a0 + HEAD_SLOT] = kpe
    v_ref[...] = kv[:, MLA_HEADS * QK_NOPE_DIM:].astype(BF16)


def _rope_slot_pair(w):
    half = QK_ROPE_DIM // 2
    pad = jnp.zeros(w.shape[:-1] + (LANES - QK_ROPE_DIM,), w.dtype)
    a = jnp.concatenate([w, pad], axis=-1)
    b = jnp.concatenate([-w[..., half:], w[..., :half], pad], axis=-1)
    return a, b


def _proj_call(x2d, pos2d, attn_norm, w_in, q_a_norm, w_q_b, kv_a_norm, w_kv_b, tm=256):
    T = x2d.shape[0]
    kpa, kpb = _rope_slot_pair(w_in[:, COL_KV:COL_KPE])
    w1 = jnp.concatenate([w_in[:, :COL_KV], kpa, kpb, w_in[:, COL_KPE:]], axis=1).astype(BF16)
    wq = w_q_b.reshape(Q_LORA_RANK, MLA_HEADS, QK_NOPE_DIM + QK_ROPE_DIM)
    qpa, qpb = _rope_slot_pair(wq[..., QK_NOPE_DIM:])
    wqb = jnp.concatenate([
        jnp.concatenate([wq[..., :QK_NOPE_DIM], qpa], axis=-1).reshape(Q_LORA_RANK, MLA_HEADS * HEAD_SLOT),
        qpb.reshape(Q_LORA_RANK, MLA_HEADS * LANES)], axis=1).astype(BF16)
    wkv = w_kv_b.reshape(KV_LORA_RANK, MLA_HEADS, QK_NOPE_DIM + V_HEAD_DIM)
    wkvb = jnp.concatenate([wkv[..., :QK_NOPE_DIM].reshape(KV_LORA_RANK, -1),
                            wkv[..., QK_NOPE_DIM:].reshape(KV_LORA_RANK, -1)], axis=1).astype(BF16)
    inv_freq = 1.0 / (ROPE_THETA ** (jnp.arange(0, QK_ROPE_DIM, 2, dtype=F32) / QK_ROPE_DIM))
    freq = jnp.concatenate([inv_freq, inv_freq, jnp.zeros((LANES - QK_ROPE_DIM,), F32)])[None, :]
    scale = (QK_NOPE_DIM + QK_ROPE_DIM) ** -0.5
    n1 = w1.shape[1]
    row = lambda w: pl.BlockSpec((tm, w), lambda i: (i, 0))
    return pl.pallas_call(
        functools.partial(_proj_body, scale=scale),
        grid=(T // tm,),
        in_specs=[row(D_MODEL), row(1), _resident((1, D_MODEL)), _resident((D_MODEL, n1)),
                  _resident((1, Q_LORA_RANK)), _resident(wqb.shape),
                  _resident((1, KV_LORA_RANK)), _resident(wkvb.shape), _resident((1, LANES))],
        out_specs=[row(MLA_HEADS * HEAD_SLOT), row(MLA_HEADS * HEAD_SLOT),
                   row(MLA_HEADS * V_HEAD_DIM), row(SSM_WIDTH)],
        out_shape=[jax.ShapeDtypeStruct((T, MLA_HEADS * HEAD_SLOT), BF16),
                   jax.ShapeDtypeStruct((T, MLA_HEADS * HEAD_SLOT), BF16),
                   jax.ShapeDtypeStruct((T, MLA_HEADS * V_HEAD_DIM), BF16),
                   jax.ShapeDtypeStruct((T, SSM_WIDTH), F32)],
        compiler_params=_cparams(("parallel",)),
        name="proj_mla_rope",
    )(x2d, pos2d, attn_norm[None, :], w1, q_a_norm[None, :], wqb, kv_a_norm[None, :], wkvb, freq)


def _attn_body(q_ref, k_ref, v_ref, o_ref, *, tq):
    i = pl.program_id(2)

    def step(j, carry, diagonal):
        off = pl.multiple_of(j * tq, tq)
        out = []
        for hd in range(ATTN_HEADS_PER_STEP):
            m, l, acc = carry[hd]
            q = q_ref[:, hd * HEAD_SLOT:(hd + 1) * HEAD_SLOT]
            kj = k_ref[pl.ds(off, tq), hd * HEAD_SLOT:(hd + 1) * HEAD_SLOT]
            vj = v_ref[pl.ds(off, tq), hd * V_HEAD_DIM:(hd + 1) * V_HEAD_DIM]
            s = lax.dot_general(q, kj, (((1,), (1,)), ((), ())), preferred_element_type=F32)
            if diagonal:
                rows = lax.broadcasted_iota(I32, s.shape, 0)
                cols = lax.broadcasted_iota(I32, s.shape, 1)
                s = jnp.where(rows >= cols, s, -jnp.inf)
            m_new = jnp.maximum(m, jnp.max(s, axis=-1, keepdims=True))
            alpha = jnp.exp(m - m_new)
            p = jnp.exp(s - m_new)
            l = alpha * l + jnp.sum(p, axis=-1, keepdims=True)
            acc = alpha * acc + jnp.dot(p.astype(BF16), vj, preferred_element_type=F32)
            out.append((m_new, l, acc))
        return tuple(out)

    init = tuple((jnp.full((tq, 1), -jnp.inf, F32), jnp.zeros((tq, 1), F32),
                  jnp.zeros((tq, V_HEAD_DIM), F32)) for _ in range(ATTN_HEADS_PER_STEP))
    carry = lax.fori_loop(0, i, lambda j, c: step(j, c, False), init)
    carry = step(i, carry, True)
    for hd in range(ATTN_HEADS_PER_STEP):
        _, l, acc = carry[hd]
        o_ref[:, hd * V_HEAD_DIM:(hd + 1) * V_HEAD_DIM] = acc / l


def _attn_call(q, k, v, batch, seq, tq=512):
    nq = seq // tq
    hp = ATTN_HEADS_PER_STEP
    return pl.pallas_call(
        functools.partial(_attn_body, tq=tq),
        grid=(batch, MLA_HEADS // hp, nq),
        in_specs=[pl.BlockSpec((tq, hp * HEAD_SLOT), lambda b, h, i: (b * nq + i, h)),
                  pl.BlockSpec((seq, hp * HEAD_SLOT), lambda b, h, i: (b, h)),
                  pl.BlockSpec((seq, hp * V_HEAD_DIM), lambda b, h, i: (b, h))],
        out_specs=pl.BlockSpec((tq, hp * V_HEAD_DIM), lambda b, h, i: (b * nq + i, h)),
        out_shape=jax.ShapeDtypeStruct((batch * seq, MLA_HEADS * V_HEAD_DIM), F32),
        compiler_params=_cparams(("parallel", "parallel", "arbitrary")),
        name="mla_attention",
    )(q, k, v)


def _s5_param_body(lr_ref, li_ref, ldt_ref, br_ref, bi_ref, abr_ref, abi_ref, bbr_ref, bbi_ref):
    lr = lr_ref[...]
    li = li_ref[...]
    dt = jnp.exp(ldt_ref[...])
    mag = jnp.exp(lr * dt)
    ab_re = mag * jnp.cos(li * dt)
    ab_im = mag * jnp.sin(li * dt)
    nr, ni = ab_re - 1.0, ab_im
    den = lr * lr + li * li
    coef_re = (nr * lr + ni * li) / den
    coef_im = (ni * lr - nr * li) / den
    abr_ref[...] = ab_re
    abi_ref[...] = ab_im
    for c in range(SSM_GROUP):
        br = br_ref[c]
        bi = bi_ref[c]
        bbr_ref[c] = coef_re * br - coef_im * bi
        bbi_ref[c] = coef_re * bi + coef_im * br


def _s5_param_call(lam_re, lam_im, log_dt, b_re, b_im):
    G, P, C = SSM_GROUPS, SSM_STATE, SSM_GROUP
    gp = jax.ShapeDtypeStruct((G, P), F32)
    cgp = jax.ShapeDtypeStruct((C, G, P), F32)
    return pl.pallas_call(
        _s5_param_body, out_shape=[gp, gp, cgp, cgp], name="s5_discretise",
    )(lam_re, lam_im, log_dt[:, None], b_re.transpose(2, 0, 1), b_im.transpose(2, 0, 1))


def _s5_body(u_ref, bre_ref, bim_ref, cre_ref, cim_ref, ar_ref, ai_ref, d_ref, wg_ref, bg_ref,
             o_ref, sre, sim, st_re, st_im, *, tt):
    nk = SSM_WIDTH // MXU_DIM
    cols_per_k = STATE_ELEMS // nk
    lane_blocks = cols_per_k // LANES
    c_idx = pl.program_id(1)

    @pl.when(c_idx == 0)
    def _():
        st_re[...] = jnp.zeros_like(st_re)
        st_im[...] = jnp.zeros_like(st_im)

    u = u_ref[...]
    ub = u.astype(BF16)
    for k in range(nk):
        uk = ub[:, k * MXU_DIM:(k + 1) * MXU_DIM]
        pr = jnp.dot(uk, bre_ref[k], preferred_element_type=F32)
        pi = jnp.dot(uk, bim_ref[k], preferred_element_type=F32)
        for m in range(lane_blocks):
            blk = k * lane_blocks + m
            s_, j_ = blk // STATE_SLABS, blk % STATE_SLABS
            sre[j_, pl.ds(s_, tt, stride=SUBLANES), :] = pr[:, m * LANES:(m + 1) * LANES]
            sim[j_, pl.ds(s_, tt, stride=SUBLANES), :] = pi[:, m * LANES:(m + 1) * LANES]

    ar = [ar_ref[j] for j in range(STATE_SLABS)]
    ai = [ai_ref[j] for j in range(STATE_SLABS)]

    def scan_step(t, carry):
        xr, xi = carry
        row = pl.multiple_of(t * SUBLANES, SUBLANES)
        nxr, nxi = [], []
        for j in range(STATE_SLABS):
            br = sre[j, pl.ds(row, SUBLANES), :]
            bi = sim[j, pl.ds(row, SUBLANES), :]
            r = ar[j] * xr[j] - ai[j] * xi[j] + br
            im = ar[j] * xi[j] + ai[j] * xr[j] + bi
            sre[j, pl.ds(row, SUBLANES), :] = r
            sim[j, pl.ds(row, SUBLANES), :] = im
            nxr.append(r)
            nxi.append(im)
        return tuple(nxr), tuple(nxi)

    x0 = (tuple(st_re[j] for j in range(STATE_SLABS)), tuple(st_im[j] for j in range(STATE_SLABS)))
    xr, xi = lax.fori_loop(0, tt, scan_step, x0, unroll=8)
    for j in range(STATE_SLABS):
        st_re[j] = xr[j]
        st_im[j] = xi[j]

    ys = []
    for k in range(nk):
        xre, xim = [], []
        for m in range(lane_blocks):
            blk = k * lane_blocks + m
            s_, j_ = blk // STATE_SLABS, blk % STATE_SLABS
            xre.append(sre[j_, pl.ds(s_, tt, stride=SUBLANES), :])
            xim.append(sim[j_, pl.ds(s_, tt, stride=SUBLANES), :])
        xre = jnp.concatenate(xre, axis=1).astype(BF16)
        xim = jnp.concatenate(xim, axis=1).astype(BF16)
        ys.append(jnp.dot(xre, cre_ref[k], preferred_element_type=F32)
                  - jnp.dot(xim, cim_ref[k], preferred_element_type=F32))
    y = jnp.concatenate(ys, axis=1) + d_ref[...] * u
    y = jax.nn.gelu(y)
    gate = jax.nn.sigmoid(jnp.dot(y.astype(BF16), wg_ref[...], preferred_element_type=F32) + bg_ref[...])
    o_ref[...] = y * gate


def _block_diag(blocks):
    nk, ng, r, c = blocks.shape
    eye = jnp.eye(ng, dtype=blocks.dtype)
    out = blocks[:, :, :, None, :] * eye[None, :, None, :, None]
    return out.reshape(nk, ng * r, ng * c)


def _s5_call(u, ab_re, ab_im, bb_re, bb_im, c_re, c_im, d_skip, w_glu, b_glu, batch, seq, tt=512):
    G, P, C = SSM_GROUPS, SSM_STATE, SSM_GROUP
    nk = SSM_WIDTH // MXU_DIM
    gk = G // nk
    def bmat(bb):
        blocks = bb.transpose(1, 0, 2).reshape(nk, gk, C, P)
        return _block_diag(blocks).astype(BF16)
    def cmat(cc):
        blocks = cc.transpose(0, 2, 1).reshape(nk, gk, P, C)
        return _block_diag(blocks).astype(BF16)
    def slab(a):
        return a.reshape(SUBLANES, STATE_SLABS, LANES).transpose(1, 0, 2)
    nc = seq // tt
    row = pl.BlockSpec((tt, SSM_WIDTH), lambda b, c: (b * nc + c, 0))
    return pl.pallas_call(
        functools.partial(_s5_body, tt=tt),
        grid=(batch, nc),
        in_specs=[row, _resident((nk, MXU_DIM, STATE_ELEMS // nk)), _resident((nk, MXU_DIM, STATE_ELEMS // nk)),
                  _resident((nk, STATE_ELEMS // nk, MXU_DIM)), _resident((nk, STATE_ELEMS // nk, MXU_DIM)),
                  _resident((STATE_SLABS, SUBLANES, LANES)), _resident((STATE_SLABS, SUBLANES, LANES)),
                  _resident((1, SSM_WIDTH)), _resident((SSM_WIDTH, SSM_WIDTH)), _resident((1, SSM_WIDTH))],
        out_specs=row,
        out_shape=jax.ShapeDtypeStruct((batch * seq, SSM_WIDTH), F32),
        scratch_shapes=[pltpu.VMEM((STATE_SLABS, tt * SUBLANES, LANES), F32),
                        pltpu.VMEM((STATE_SLABS, tt * SUBLANES, LANES), F32),
                        pltpu.VMEM((STATE_SLABS, SUBLANES, LANES), F32),
                        pltpu.VMEM((STATE_SLABS, SUBLANES, LANES), F32)],
        compiler_params=_cparams(("parallel", "arbitrary")),
        name="s5_scan_glu",
    )(u, bmat(bb_re), bmat(bb_im), cmat(c_re), cmat(c_im), slab(ab_re), slab(ab_im),
      d_skip[None, :], w_glu.astype(BF16), b_glu[None, :])


def _mix_body(x_ref, ao_ref, so_ref, ga_ref, gs_ref, woa_ref, wos_ref, gx_ref, wxq_ref,
              x1_ref, qx_ref, *, xscale):
    a = _rms(ao_ref[...], ga_ref[...]).astype(BF16)
    s = _rms(so_ref[...], gs_ref[...]).astype(BF16)
    x1 = (x_ref[...] + jnp.dot(a, woa_ref[...], preferred_element_type=F32)
          + jnp.dot(s, wos_ref[...], preferred_element_type=F32))
    x1_ref[...] = x1
    hq = _rms(x1, gx_ref[...]).astype(BF16)
    qx_ref[...] = (jnp.dot(hq, wxq_ref[...], preferred_element_type=F32) * xscale).astype(BF16)


def _mix_call(x2d, attn_o, ssm_o, g_attn, g_ssm, w_out, g_x, w_xq, tm=256):
    T = x2d.shape[0]
    na = MLA_HEADS * V_HEAD_DIM
    assert na == SSM_WIDTH
    w_out_bf = w_out.astype(BF16)
    row = lambda w: pl.BlockSpec((tm, w), lambda i: (i, 0))
    return pl.pallas_call(
        functools.partial(_mix_body, xscale=XATTN_HEAD_DIM ** -0.5),
        grid=(T // tm,),
        in_specs=[row(D_MODEL), row(na), row(SSM_WIDTH), _resident((1, na)), _resident((1, SSM_WIDTH)),
                  pl.BlockSpec((na, D_MODEL), lambda i: (0, 0), pipeline_mode=pl.Buffered(1)),
                  pl.BlockSpec((SSM_WIDTH, D_MODEL), lambda i: (1, 0), pipeline_mode=pl.Buffered(1)),
                  _resident((1, D_MODEL)), _resident((D_MODEL, D_MODEL))],
        out_specs=[row(D_MODEL), row(D_MODEL)],
        out_shape=[jax.ShapeDtypeStruct((T, D_MODEL), F32), jax.ShapeDtypeStruct((T, D_MODEL), BF16)],
        compiler_params=_cparams(("parallel",)),
        name="mix_out_proj",
    )(x2d, attn_o, ssm_o, g_attn[None, :], g_ssm[None, :], w_out_bf, w_out_bf, g_x[None, :],
      w_xq.astype(BF16))


def _memkv_body(m_ref, g_ref, wk_ref, wv_ref, k_ref, v_ref):
    h = _rms(m_ref[...], g_ref[...]).astype(BF16)
    k_ref[...] = jnp.dot(h, wk_ref[...], preferred_element_type=F32).astype(BF16)
    v_ref[...] = jnp.dot(h, wv_ref[...], preferred_element_type=F32).astype(BF16)


def _memkv_call(mem2d, g_mem, w_xk, w_xv, tm=256):
    R = mem2d.shape[0]
    row = pl.BlockSpec((tm, D_MODEL), lambda i: (i, 0))
    return pl.pallas_call(
        _memkv_body,
        grid=(R // tm,),
        in_specs=[row, _resident((1, D_MODEL)), _resident((D_MODEL, D_MODEL)), _resident((D_MODEL, D_MODEL))],
        out_specs=[row, row],
        out_shape=[jax.ShapeDtypeStruct((R, D_MODEL), BF16)] * 2,
        compiler_params=_cparams(("parallel",)),
        name="mem_kv_proj",
    )(mem2d, g_mem[None, :], w_xk.astype(BF16), w_xv.astype(BF16))


def _pack_bf16_pairs(xf):
    c = xf.shape[1] // 2
    bits = pltpu.bitcast(xf.astype(BF16).astype(F32), jnp.uint32)
    return lax.shift_right_logical(bits[:, :c], jnp.uint32(16)) | bits[:, c:]


def _unpack_bf16_pairs(w):
    lo = pltpu.bitcast(lax.shift_left(w, jnp.uint32(16)), F32).astype(BF16)
    hi = pltpu.bitcast(w & jnp.uint32(0xFFFF0000), F32).astype(BF16)
    return lo, hi


def _xattn_body(q_ref, k_ref, v_ref, x1_ref, wo_ref, gf_ref, x2_ref, xn_ref, xp_ref):
    outs = []
    for h in range(XATTN_HEADS):
        sl = slice(h * XATTN_HEAD_DIM, (h + 1) * XATTN_HEAD_DIM)
        s = lax.dot_general(q_ref[:, sl], k_ref[:, sl], (((1,), (1,)), ((), ())),
                            preferred_element_type=F32)
        p = jnp.exp(s - jnp.max(s, axis=-1, keepdims=True))
        denom = jnp.sum(p, axis=-1, keepdims=True)
        o = jnp.dot(p.astype(BF16), v_ref[:, sl], preferred_element_type=F32) / denom
        outs.append(o.astype(BF16))
    o = jnp.concatenate(outs, axis=1)
    x2 = x1_ref[...] + jnp.dot(o, wo_ref[...], preferred_element_type=F32)
    x2_ref[...] = x2
    xn = _rms(x2, gf_ref[...])
    xn_ref[...] = xn
    packed = _pack_bf16_pairs(xn)
    tq = packed.shape[0]
    for s in range(SUBLANES):
        xp_ref[pl.ds(s, tq, stride=SUBLANES), :] = packed[:, s * LANES:(s + 1) * LANES]


def _xattn_call(qx, kx, vx, x1, w_xo, g_ffn, batch, seq, tq=256):
    nq = seq // tq
    row = pl.BlockSpec((tq, D_MODEL), lambda b, i: (b * nq + i, 0))
    half = pl.BlockSpec((tq * SUBLANES, LANES), lambda b, i: (b * nq + i, 0))
    memb = pl.BlockSpec((MEM_LEN, D_MODEL), lambda b, i: (b, 0))
    return pl.pallas_call(
        _xattn_body,
        grid=(batch, nq),
        in_specs=[row, memb, memb, row, _resident((D_MODEL, D_MODEL)), _resident((1, D_MODEL))],
        out_specs=[row, row, half],
        out_shape=[jax.ShapeDtypeStruct((batch * seq, D_MODEL), F32),
                   jax.ShapeDtypeStruct((batch * seq, D_MODEL), F32),
                   jax.ShapeDtypeStruct((batch * seq * SUBLANES, LANES), jnp.uint32)],
        compiler_params=_cparams(("parallel", "parallel")),
        name="cross_attention",
    )(qx, kx, vx, x1, w_xo.astype(BF16), g_ffn[None, :])


def _router_body(xn_ref, wh_ref, wl_ref, br_ref, idx_ref, gate_ref, rank_ref, cnt_ref, carry, *, tm):
    @pl.when(pl.program_id(0) == 0)
    def _():
        carry[...] = jnp.zeros_like(carry)

    x = xn_ref[...]
    xh = x.astype(BF16)
    xl = (x - xh.astype(F32)).astype(BF16)
    wh = wh_ref[...]
    logits = (jnp.dot(xh, wh, preferred_element_type=F32)
              + jnp.dot(xl, wh, preferred_element_type=F32)
              + jnp.dot(xh, wl_ref[...], preferred_element_type=F32)) + br_ref[...]
    lane = lax.broadcasted_iota(I32, logits.shape, 1)
    work = logits
    sels, vals, ids = [], [], []
    for _ in range(TOP_K):
        mx = jnp.max(work, axis=-1, keepdims=True)
        am = jnp.min(jnp.where(work == mx, lane, N_EXPERTS), axis=-1, keepdims=True)
        sel = lane == am
        sels.append(sel)
        vals.append(mx)
        ids.append(am)
        work = jnp.where(sel, -jnp.inf, work)
    es = [jnp.exp(v - vals[0]) for v in vals]
    denom = es[0] + es[1] + es[2] + es[3]
    member = jnp.zeros(logits.shape, F32)
    for sel in sels:
        member = member + jnp.where(sel, 1.0, 0.0)
    r_i = lax.broadcasted_iota(I32, (tm, tm), 0)
    c_i = lax.broadcasted_iota(I32, (tm, tm), 1)
    tri = jnp.where(r_i > c_i, 1.0, 0.0).astype(BF16)
    before = jnp.dot(tri, member.astype(BF16), preferred_element_type=F32) + carry[...]
    out_lane = lax.broadcasted_iota(I32, (tm, LANES), 1)
    idx_o = jnp.zeros((tm, LANES), I32)
    gate_o = jnp.zeros((tm, LANES), F32)
    rank_o = jnp.zeros((tm, LANES), I32)
    for k in range(TOP_K):
        rk = jnp.sum(jnp.where(sels[k], before, 0.0), axis=-1, keepdims=True).astype(I32)
        idx_o = jnp.where(out_lane == k, ids[k], idx_o)
        gate_o = jnp.where(out_lane == k, es[k] / denom, gate_o)
        rank_o = jnp.where(out_lane == k, rk, rank_o)
    idx_ref[...] = idx_o
    gate_ref[...] = gate_o
    rank_ref[...] = rank_o
    carry[...] = carry[...] + jnp.sum(member, axis=0, keepdims=True)
    cnt_ref[...] = carry[...]


def _router_call(xn, w_router, b_router, tm=256):
    T = xn.shape[0]
    wide = pl.BlockSpec((tm, LANES), lambda i: (i, 0))
    w_hi = w_router.astype(BF16)
    w_lo = (w_router - w_hi.astype(F32)).astype(BF16)
    return pl.pallas_call(
        functools.partial(_router_body, tm=tm),
        grid=(T // tm,),
        in_specs=[pl.BlockSpec((tm, D_MODEL), lambda i: (i, 0)),
                  _resident((D_MODEL, N_EXPERTS)), _resident((D_MODEL, N_EXPERTS)),
                  _resident((1, N_EXPERTS))],
        out_specs=[wide, wide, wide, pl.BlockSpec((1, N_EXPERTS), lambda i: (0, 0))],
        out_shape=[jax.ShapeDtypeStruct((T, LANES), I32), jax.ShapeDtypeStruct((T, LANES), F32),
                   jax.ShapeDtypeStruct((T, LANES), I32), jax.ShapeDtypeStruct((1, N_EXPERTS), F32)],
        scratch_shapes=[pltpu.VMEM((1, N_EXPERTS), F32)],
        compiler_params=_cparams(("arbitrary",)),
        name="moe_router",
    )(xn, w_hi, w_lo, b_router[None, :])


def _dispatch_body(fill_ref, dest_ref, xp_ref, xs_hbm, zbuf, sems, *, tm):
    @pl.when(pl.program_id(0) == 0)
    def _():
        zbuf[...] = jnp.zeros_like(zbuf)

        def zero_copy(e):
            dst = pl.multiple_of(fill_ref[e] * PACKED_TILE_ROWS, PACKED_TILE_ROWS)
            return pltpu.make_async_copy(zbuf, xs_hbm.at[pl.ds(dst, PACKED_TILE_ROWS)], sems.at[0])

        def start(e, c):
            @pl.when(fill_ref[e] >= 0)
            def _():
                zero_copy(e).start()
            return c

        def wait(e, c):
            @pl.when(fill_ref[e] >= 0)
            def _():
                zero_copy(e).wait()
            return c

        lax.fori_loop(0, 2 * N_EXPERTS, start, 0)
        lax.fori_loop(0, 2 * N_EXPERTS, wait, 0)

    def row_copy(r, dst):
        src = pl.multiple_of(r * SUBLANES, SUBLANES)
        dst = pl.multiple_of(dst * SUBLANES, SUBLANES)
        return pltpu.make_async_copy(xp_ref.at[pl.ds(src, SUBLANES)], xs_hbm.at[pl.ds(dst, SUBLANES)],
                                     sems.at[1])

    def issue(r, c):
        for k in range(TOP_K):
            row_copy(r, dest_ref[0, 0, r * TOP_K + k]).start()
        return c

    lax.fori_loop(0, tm, issue, 0)

    def drain(r, c):
        for k in range(TOP_K):
            row_copy(0, 0).wait()
        return c

    lax.fori_loop(0, tm, drain, 0)


def _dispatch_call(fill_tile, dest3, xnp, n_slots):
    tm = COMBINE_ROWS
    T = xnp.shape[0] // SUBLANES
    grid_spec = pltpu.PrefetchScalarGridSpec(
        num_scalar_prefetch=1,
        grid=(T // tm,),
        in_specs=[pl.BlockSpec((1, 1, tm * TOP_K), lambda i, fl: (i, 0, 0), memory_space=pltpu.SMEM),
                  pl.BlockSpec((tm * SUBLANES, LANES), lambda i, fl: (i, 0))],
        out_specs=pl.BlockSpec(memory_space=pl.ANY),
        scratch_shapes=[pltpu.VMEM((PACKED_TILE_ROWS, LANES), jnp.uint32), pltpu.SemaphoreType.DMA((2,))],
    )
    return pl.pallas_call(
        functools.partial(_dispatch_body, tm=tm),
        grid_spec=grid_spec,
        out_shape=jax.ShapeDtypeStruct((n_slots * SUBLANES, LANES), jnp.uint32),
        compiler_params=_cparams(("arbitrary",)),
        name="moe_dispatch",
    )(fill_tile, dest3, xnp)


def _expert_body(ue_ref, ur_ref, us_ref, tail_ref, xs_hbm, wg_ref, wl_ref, bg_ref, bl_ref,
                 wd_ref, bd_ref, out_hbm, raw, acc, wg_bf, wl_bf, wd_bf, sems):
    u = pl.program_id(0)
    j = pl.program_id(1)
    rows = ur_ref[u]
    s0 = us_ref[u]
    nsub = (rows + (MOE_ROW_TILE - 1)) // MOE_ROW_TILE
    half = D_MODEL // 2

    def in_copy(first_slot, r):
        off = pl.multiple_of(r * PACKED_TILE_ROWS, PACKED_TILE_ROWS)
        src = pl.multiple_of((first_slot + r * MOE_ROW_TILE) * SUBLANES, PACKED_TILE_ROWS)
        return pltpu.make_async_copy(xs_hbm.at[pl.ds(src, PACKED_TILE_ROWS)],
                                     raw.at[pl.ds(off, PACKED_TILE_ROWS)], sems.at[0])

    def fetch_rows(first_slot, n_tiles):
        def start(r, c):
            in_copy(first_slot, r).start()
            return c

        lax.fori_loop(0, n_tiles, start, 0)

    @pl.when(jnp.logical_and(j == 0, rows > 0))
    def _():
        @pl.when(u == 0)
        def _():
            fetch_rows(s0, nsub)

        bd = jnp.broadcast_to(bd_ref[0], (MOE_ROW_TILE, D_MODEL))

        def init(r, c):
            off = pl.multiple_of(r * MOE_ROW_TILE, MOE_ROW_TILE)
            acc[pl.ds(off, MOE_ROW_TILE), :] = bd
            return c

        def wait(r, c):
            in_copy(s0, r).wait()
            return c

        lax.fori_loop(0, nsub, init, 0)
        lax.fori_loop(0, nsub, wait, 0)

    bg = bg_ref[0]
    bl = bl_ref[0]

    last_step = j == MOE_NJ - 1
    pair = 2 * MOE_ROW_TILE
    npair = nsub // 2
    odd = nsub - 2 * npair

    def up(off, n, wg, wl):
        base = off * SUBLANES
        parts = [_unpack_bf16_pairs(raw[pl.ds(base + s, n, stride=SUBLANES), :]) for s in range(SUBLANES)]
        lo = jnp.concatenate([p[0] for p in parts], axis=1)
        hi = jnp.concatenate([p[1] for p in parts], axis=1)
        g = (jnp.dot(lo, wg[:half], preferred_element_type=F32)
             + jnp.dot(hi, wg[half:], preferred_element_type=F32) + bg)
        l = (jnp.dot(lo, wl[:half], preferred_element_type=F32)
             + jnp.dot(hi, wl[half:], preferred_element_type=F32) + bl)
        glu = jnp.minimum(g, SWIGLU_LIMIT)
        lin = jnp.clip(l, -SWIGLU_LIMIT, SWIGLU_LIMIT)
        return (glu * jax.nn.sigmoid(SWIGLU_ALPHA * glu) * (lin + 1.0)).astype(BF16)

    def down(off, n, act, wd):
        acc[pl.ds(off, n), :] += jnp.dot(act, wd, preferred_element_type=F32)

    def out_copy(off, n):
        dst = pl.multiple_of(s0 + off, MOE_ROW_TILE)
        return pltpu.make_async_copy(acc.at[pl.ds(off, n)], out_hbm.at[pl.ds(dst, n)], sems.at[2])

    def send(off, n):
        @pl.when(last_step)
        def _():
            out_copy(off, n).start()

    def cast_weights():
        wg = wg_ref[0].astype(BF16)
        wl = wl_ref[0].astype(BF16)
        wd = wd_ref[0].astype(BF16)
        wg_bf[...] = wg
        wl_bf[...] = wl
        wd_bf[...] = wd
        return wg, wl

    @pl.when(npair > 0)
    def _():
        wg, wl = cast_weights()
        act0 = up(0, pair, wg, wl)

        def body(r, act):
            off = pl.multiple_of(r * pair, pair)
            nxt = pl.multiple_of(off + pair, pair)
            down(off, pair, act, wd_bf[...])
            act = up(nxt, pair, wg_bf[...], wl_bf[...])
            send(off, pair)
            return act

        act_last = lax.fori_loop(0, npair - 1, body, act0)
        off = pl.multiple_of((npair - 1) * pair, pair)
        down(off, pair, act_last, wd_bf[...])
        send(off, pair)

    @pl.when(odd > 0)
    def _():
        @pl.when(npair == 0)
        def _():
            cast_weights()

        off = pl.multiple_of(npair * pair, pair)
        down(off, MOE_ROW_TILE, up(off, MOE_ROW_TILE, wg_bf[...], wl_bf[...]), wd_bf[...])
        send(off, MOE_ROW_TILE)

    nxt_u = jnp.minimum(u + 1, pl.num_programs(0) - 1)
    nxt_rows = jnp.where(u + 1 < pl.num_programs(0), ur_ref[nxt_u], 0)

    @pl.when(jnp.logical_and(last_step, nxt_rows > 0))
    def _():
        fetch_rows(us_ref[nxt_u], (nxt_rows + (MOE_ROW_TILE - 1)) // MOE_ROW_TILE)

    @pl.when(last_step)
    def _():
        def wait_pair(r, c):
            out_copy(0, pair).wait()
            return c

        lax.fori_loop(0, npair, wait_pair, 0)

        @pl.when(odd > 0)
        def _():
            out_copy(0, MOE_ROW_TILE).wait()


    @pl.when(jnp.logical_and(u == pl.num_programs(0) - 1, j == MOE_NJ - 1))
    def _():
        acc[pl.ds(0, MOE_ROW_TILE), :] = jnp.zeros((MOE_ROW_TILE, D_MODEL), F32)
        first = tail_ref[0]
        count = tail_ref[1]

        def zero_copy(t):
            dst = pl.multiple_of((first + t) * MOE_ROW_TILE, MOE_ROW_TILE)
            return pltpu.make_async_copy(acc.at[pl.ds(0, MOE_ROW_TILE)],
                                         out_hbm.at[pl.ds(dst, MOE_ROW_TILE)], sems.at[2])

        def start(t, c):
            zero_copy(t).start()
            return c

        def wait(t, c):
            zero_copy(t).wait()
            return c

        lax.fori_loop(0, count, start, 0)
        lax.fori_loop(0, count, wait, 0)


def _expert_call(unit_e, unit_rows, unit_s0, tail, xs, w_up, b_up, w_down, b_down, n_slots, n_units):
    fc = MOE_FF_CHUNK
    grid_spec = pltpu.PrefetchScalarGridSpec(
        num_scalar_prefetch=4,
        grid=(n_units, MOE_NJ),
        in_specs=[
            pl.BlockSpec(memory_space=pl.ANY),
            pl.BlockSpec((1, D_MODEL, fc), lambda u, j, ue, ur, us, tl: (ue[u], 0, j)),
            pl.BlockSpec((1, D_MODEL, fc), lambda u, j, ue, ur, us, tl: (ue[u], 0, MOE_NJ + j)),
            pl.BlockSpec((1, 1, fc), lambda u, j, ue, ur, us, tl: (ue[u], 0, j)),
            pl.BlockSpec((1, 1, fc), lambda u, j, ue, ur, us, tl: (ue[u], 0, MOE_NJ + j)),
            pl.BlockSpec((1, fc, D_MODEL), lambda u, j, ue, ur, us, tl: (ue[u], j, 0)),
            pl.BlockSpec((1, 1, D_MODEL), lambda u, j, ue, ur, us, tl: (ue[u], 0, 0)),
        ],
        out_specs=pl.BlockSpec(memory_space=pl.ANY),
        scratch_shapes=[
            pltpu.VMEM((MOE_UNIT_ROWS * SUBLANES, LANES), jnp.uint32),
            pltpu.VMEM((MOE_UNIT_ROWS, D_MODEL), F32),
            pltpu.VMEM((D_MODEL, fc), BF16),
            pltpu.VMEM((D_MODEL, fc), BF16),
            pltpu.VMEM((fc, D_MODEL), BF16),
            pltpu.SemaphoreType.DMA((3,)),
        ],
    )
    return pl.pallas_call(
        _expert_body,
        grid_spec=grid_spec,
        out_shape=jax.ShapeDtypeStruct((n_slots, D_MODEL), F32),
        compiler_params=_cparams(("arbitrary", "arbitrary")),
        name="moe_experts",
    )(unit_e, unit_rows, unit_s0, tail, xs, w_up, w_up, b_up[:, None, :], b_up[:, None, :],
      w_down, b_down[:, None, :])


def _combine_body(dest_ref, next_ref, gate_ref, x2_ref, gf_ref, out_hbm, o_ref, buf, sems, *, tm):
    i = pl.program_id(0)
    n = pl.num_programs(0)
    slot = lax.rem(i, 2)

    def row_copy(sl, r, k, src):
        return pltpu.make_async_copy(out_hbm.at[pl.ds(src, 1)], buf.at[sl, k, pl.ds(r, 1)], sems.at[sl])

    def issue_tile(table, sl):
        def issue(r, c):
            for k in range(TOP_K):
                row_copy(sl, r, k, table[0, 0, r * TOP_K + k]).start()
            return c

        lax.fori_loop(0, tm, issue, 0)

    @pl.when(i == 0)
    def _():
        issue_tile(dest_ref, 0)

    @pl.when(i + 1 < n)
    def _():
        issue_tile(next_ref, 1 - slot)

    def drain(r, c):
        for k in range(TOP_K):
            row_copy(slot, 0, k, 0).wait()
        return c

    lax.fori_loop(0, tm, drain, 0)
    gates = gate_ref[...]
    y = gates[:, 0:1] * buf[slot, 0]
    for k in range(1, TOP_K):
        y = y + gates[:, k:k + 1] * buf[slot, k]
    o_ref[...] = _rms(x2_ref[...] + y, gf_ref[...])


def _combine_call(dest3, gates, x2, final_norm, expert_out):
    tm = COMBINE_ROWS
    T = x2.shape[0]
    n = T // tm
    row = pl.BlockSpec((tm, D_MODEL), lambda i: (i, 0))
    table = lambda f: pl.BlockSpec((1, 1, tm * TOP_K), f, memory_space=pltpu.SMEM)
    return pl.pallas_call(
        functools.partial(_combine_body, tm=tm),
        grid=(n,),
        in_specs=[table(lambda i: (i, 0, 0)), table(lambda i: (jnp.minimum(i + 1, n - 1), 0, 0)),
                  pl.BlockSpec((tm, LANES), lambda i: (i, 0)), row, _resident((1, D_MODEL)),
                  pl.BlockSpec(memory_space=pl.ANY)],
        out_specs=row,
        out_shape=jax.ShapeDtypeStruct((T, D_MODEL), F32),
        scratch_shapes=[pltpu.VMEM((2, TOP_K, tm, D_MODEL), F32), pltpu.SemaphoreType.DMA((2,))],
        compiler_params=_cparams(("arbitrary",)),
        name="moe_combine_norm",
    )(dest3, dest3, gates, x2, final_norm[None, :], expert_out)


def _routing_tables(idx, rank, counts, n_tokens):
    tk = n_tokens * TOP_K
    padded = (counts + MOE_ROW_TILE - 1) // MOE_ROW_TILE * MOE_ROW_TILE
    pad_end = jnp.cumsum(padded)
    pad_start = pad_end - padded
    onehot = idx[:, :, None] == jnp.arange(N_EXPERTS, dtype=I32)[None, None, :]
    dest = jnp.sum(jnp.where(onehot, pad_start[None, None, :], 0), axis=-1) + rank
    n_slots = (tk // MOE_ROW_TILE + N_EXPERTS) * MOE_ROW_TILE
    n_tiles = n_slots // MOE_ROW_TILE
    part_tile = jnp.where(counts % MOE_ROW_TILE != 0, pad_end // MOE_ROW_TILE - 1, -1)
    spare = pad_end[-1] // MOE_ROW_TILE + jnp.arange(N_EXPERTS, dtype=I32)
    fill_tile = jnp.concatenate([part_tile, jnp.where(spare < n_tiles, spare, -1)]).astype(I32)
    n_units = N_EXPERTS + tk // MOE_UNIT_ROWS
    units_per_e = (counts + MOE_UNIT_ROWS - 1) // MOE_UNIT_ROWS
    unit_end = jnp.cumsum(units_per_e)
    n_active = unit_end[-1]
    uid = jnp.arange(n_units, dtype=I32)
    ue = jnp.minimum(jnp.sum(uid[:, None] >= unit_end[None, :], axis=1), N_EXPERTS - 1).astype(I32)
    k_in_e = uid - (unit_end - units_per_e)[ue]
    rows = jnp.clip(counts[ue] - k_in_e * MOE_UNIT_ROWS, 0, MOE_UNIT_ROWS)
    active = uid < n_active
    last_e = ue[jnp.maximum(n_active - 1, 0)]
    unit_e = jnp.where(active, ue, last_e).astype(I32)
    unit_rows = jnp.where(active, rows, 0).astype(I32)
    unit_s0 = jnp.where(active, pad_start[ue] + k_in_e * MOE_UNIT_ROWS, 0).astype(I32)
    used_tiles = pad_end[-1] // MOE_ROW_TILE
    tail = jnp.stack([used_tiles, n_slots // MOE_ROW_TILE - used_tiles]).astype(I32)
    return dest, fill_tile, unit_e, unit_rows, unit_s0, tail, n_slots, n_units


def kernel(x, mem, positions, attn_norm, w_in, q_a_norm, w_q_b, kv_a_norm, w_kv_b, ssm_lambda_re, ssm_lambda_im, ssm_log_dt, ssm_b_re, ssm_b_im, ssm_c_re, ssm_c_im, ssm_d, w_glu, b_glu, mix_norm_attn, mix_norm_ssm, w_out, xattn_norm, mem_norm, w_xq, w_xk, w_xv, w_xo, ffn_norm, w_router, b_router, w_up, b_up, w_down, b_down, final_norm):
    B, S, D = x.shape
    T = B * S
    assert w_in.shape[0] == 1, "single-layer block"
    l = 0
    xt = x.reshape(T, D)
    pos2d = positions.reshape(T, 1)
    mem2d = mem.reshape(B * MEM_LEN, D)
    q, k, v, u = _proj_call(xt, pos2d, attn_norm[l], w_in[l], q_a_norm[l], w_q_b[l],
                            kv_a_norm[l], w_kv_b[l])
    attn_o = _attn_call(q, k, v, B, S)
    ab_re, ab_im, bb_re, bb_im = _s5_param_call(ssm_lambda_re[l], ssm_lambda_im[l], ssm_log_dt[l],
                                                ssm_b_re[l], ssm_b_im[l])
    ssm_o = _s5_call(u, ab_re, ab_im, bb_re, bb_im, ssm_c_re[l], ssm_c_im[l], ssm_d[l],
                     w_glu[l], b_glu[l], B, S)
    x1, qx = _mix_call(xt, attn_o, ssm_o, mix_norm_attn[l], mix_norm_ssm[l], w_out[l],
                       xattn_norm[l], w_xq[l])
    kx, vx = _memkv_call(mem2d, mem_norm[l], w_xk[l], w_xv[l])
    x2, xn, xnp = _xattn_call(qx, kx, vx, x1, w_xo[l], ffn_norm[l], B, S)
    idx_w, gate_w, rank_w, counts_f = _router_call(xn, w_router[l], b_router[l])
    counts = counts_f[0].astype(I32)
    dest, fill_tile, unit_e, unit_rows, unit_s0, tail, n_slots, n_units = _routing_tables(
        idx_w[:, :TOP_K], rank_w[:, :TOP_K], counts, T)
    dest3 = dest.reshape(-1, 1, COMBINE_ROWS * TOP_K)
    xs = _dispatch_call(fill_tile, dest3, xnp, n_slots)
    expert_out = _expert_call(unit_e, unit_rows, unit_s0, tail, xs,
                              w_up[l], b_up[l], w_down[l], b_down[l], n_slots, n_units)
    out = _combine_call(dest3, gate_w, x2, final_norm, expert_out)
    return out.reshape(B, S, D)
```
